```python
import math
import jax
import jax.numpy as jnp
from jax import lax
import numpy as np

D_MODEL = 1024
BATCH = 4
SEQ = 8192
DEPTH = 2
DEC_BATCH = 32
DEC_SEQ = 8
PAST_LEN = 16384
PAGE_SIZE = 128

N_META = 16
CONV_DIM = D_MODEL // 2
CONV_W = 3
N_HEADS = 4
HEAD_DIM = D_MODEL // (4 * N_HEADS)
ATTN_DIM = 2 * N_HEADS * HEAD_DIM
MIX_DIM = CONV_DIM + ATTN_DIM
IN_DIM = 3 * CONV_DIM + 3 * ATTN_DIM
D_FF = -(-8 * D_MODEL // (3 * 256)) * 256
ROPE_THETA = 10000.0
Q_BLOCK = 128
ALPHA = (2 * DEPTH) ** 0.25
BETA = (8 * DEPTH) ** -0.25
LN_EPS = 1e-5
NEG_INF = -1e30

kernel_name = 'hybrid_shortconv_diffattn_decoder_step'


def _layernorm(x, g, b):
    xf = x.astype(jnp.float32)
    mu = jnp.mean(xf, axis=-1, keepdims=True)
    var = jnp.mean(jnp.square(xf - mu), axis=-1, keepdims=True)
    return ((xf - mu) * lax.rsqrt(var + LN_EPS) * g.astype(jnp.float32) + b.astype(jnp.float32)).astype(x.dtype)


def _rope(x, pos):
    inv = 1.0 / (ROPE_THETA ** (jnp.arange(0, HEAD_DIM, 2, dtype=jnp.float32) / HEAD_DIM))
    ang = pos.astype(jnp.float32)[:, None] * inv[None, :]
    ang = jnp.concatenate([ang, ang], axis=-1)[:, None, :]
    x1, x2 = jnp.split(x, 2, axis=-1)
    rot = jnp.concatenate([-x2, x1], axis=-1)
    return (x.astype(jnp.float32) * jnp.cos(ang) + rot.astype(jnp.float32) * jnp.sin(ang)).astype(x.dtype)


def _diff_attend(q, k, v, q_pos, k_pos, lam, sub_g, lam_init):
    bsz, nq = q.shape[0], q.shape[1]
    s = jnp.einsum('bqhd,bshd->bhqs', q, k).astype(jnp.float32) * (HEAD_DIM ** -0.5)
    mask = k_pos[None, :] <= q_pos[:, None]
    p = jax.nn.softmax(jnp.where(mask[None, None], s, NEG_INF), axis=-1)
    p = p.reshape(bsz, N_HEADS, 2, nq, s.shape[-1])
    a = p[:, :, 0] - lam * p[:, :, 1]
    o = jnp.einsum('bhqs,bshe->bqhe', a.astype(v.dtype), v).astype(jnp.float32)
    o = o * lax.rsqrt(jnp.mean(jnp.square(o), axis=-1, keepdims=True) + LN_EPS)
    o = o * sub_g.astype(jnp.float32) * (1.0 - lam_init)
    return o.reshape(bsz, nq, ATTN_DIM).astype(v.dtype)


def _prompt_attend(q, k, v, pos, lam, sub_g, lam_init, l):
    meta = _diff_attend(q[:, :N_META], k[:, :N_META], v[:, :N_META], pos[:N_META], pos[:N_META],
                        lam, sub_g, lam_init)
    bsz, t = q.shape[0], q.shape[1]
    nb = (t - N_META) // Q_BLOCK
    qb = q[:, N_META:].reshape(bsz, nb, Q_BLOCK, 2 * N_HEADS, HEAD_DIM).swapaxes(0, 1)
    starts = N_META + jnp.arange(nb, dtype=jnp.int32) * Q_BLOCK

    def one(args):
        qi, s0 = args
        return _diff_attend(qi, k, v, s0 + jnp.arange(Q_BLOCK, dtype=jnp.int32), pos, lam, sub_g, lam_init)

    out = lax.map(one, (qb, starts))
    real = out.swapaxes(0, 1).reshape(bsz, nb * Q_BLOCK, ATTN_DIM)
    return jnp.concatenate([meta, real], axis=1)


def _layer(x, pos, conv_prefix, attend, l, w_in, conv_w, w_out, lambda_q1, lambda_k1, lambda_q2,
           lambda_k2, subln_g, ln1_g, ln1_b, ln2_g, ln2_b, w_gate_up, w_down):
    bsz, t, _ = x.shape
    z = x @ w_in[l]
    u_in, b_g, c_g, q, k, v = jnp.split(
        z, [CONV_DIM, 2 * CONV_DIM, 3 * CONV_DIM, 3 * CONV_DIM + ATTN_DIM, 3 * CONV_DIM + 2 * ATTN_DIM], axis=-1)
    u = c_g * u_in
    u_full = jnp.concatenate([conv_prefix.astype(u.dtype), u], axis=1)
    cw = conv_w[l].astype(u.dtype)
    conv = u_full[:, 0:t] * cw[0]
    for j in range(1, CONV_W):
        conv = conv + u_full[:, j:j + t] * cw[j]
    y_conv = b_g * conv
    conv_state = u_full[:, -(CONV_W - 1):]
    q = _rope(q.reshape(bsz, t, 2 * N_HEADS, HEAD_DIM), pos)
    k = _rope(k.reshape(bsz, t, 2 * N_HEADS, HEAD_DIM), pos)
    v = v.reshape(bsz, t, N_HEADS, 2 * HEAD_DIM)
    lam_init = 0.8 - 0.6 * math.exp(-0.3 * l)
    f32 = jnp.float32
    lam = (jnp.exp(jnp.sum(lambda_q1[l].astype(f32) * lambda_k1[l].astype(f32)))
           - jnp.exp(jnp.sum(lambda_q2[l].astype(f32) * lambda_k2[l].astype(f32))) + lam_init)
    y_attn = attend(q, k, v, pos, lam, subln_g[l], lam_init, l)
    mix = jnp.concatenate([y_conv, y_attn], axis=-1) @ w_out[l]
    x = _layernorm(ALPHA * x + mix, ln1_g[l], ln1_b[l])
    g, up = jnp.split(x @ w_gate_up[l], 2, axis=-1)
    x = _layernorm(ALPHA * x + (jax.nn.silu(g) * up) @ w_down[l], ln2_g[l], ln2_b[l])
    return x, k, v, conv_state


def setup_inputs(seed: int = 0) -> dict:
    key = jax.random.key(seed)
    ks = jax.random.split(key, 24)
    n_pages = PAST_LEN // PAGE_SIZE
    n_used = DEC_BATCH * n_pages
    n_pool = (5 * n_used + 3) // 4
    nrm = jax.random.normal
    f32 = jnp.float32
    x_prompt = nrm(ks[0], (BATCH, SEQ, D_MODEL), f32)
    x_sample = nrm(ks[1], (DEC_BATCH, DEC_SEQ, D_MODEL), f32)
    cache_k = nrm(ks[2], (DEPTH, n_pool, PAGE_SIZE, 2 * N_HEADS, HEAD_DIM), f32)
    cache_v = nrm(ks[3], (DEPTH, n_pool, PAGE_SIZE, N_HEADS, 2 * HEAD_DIM), f32)
    state_conv = nrm(ks[4], (DEPTH, DEC_BATCH, CONV_W - 1, CONV_DIM), f32)
    page_table = jax.random.permutation(ks[5], n_pool)[:n_used].reshape(DEC_BATCH, n_pages).astype(jnp.int32)
    meta_tokens = nrm(ks[6], (N_META, D_MODEL), f32)
    s_in = D_MODEL ** -0.5
    w_in_rest = nrm(ks[7], (DEPTH, D_MODEL, IN_DIM - ATTN_DIM), f32) * s_in
    w_in_v = nrm(ks[8], (DEPTH, D_MODEL, ATTN_DIM), f32) * (s_in * BETA)
    w_in = jnp.concatenate([w_in_rest, w_in_v], axis=-1)
    conv_w = nrm(ks[9], (DEPTH, CONV_W, CONV_DIM), f32) * (CONV_W ** -0.5)
    w_out = nrm(ks[10], (DEPTH, MIX_DIM, D_MODEL), f32) * (MIX_DIM ** -0.5 * BETA)
    lambda_q1 = nrm(ks[11], (DEPTH, HEAD_DIM), f32) * 0.1
    lambda_k1 = nrm(ks[12], (DEPTH, HEAD_DIM), f32) * 0.1
    lambda_q2 = nrm(ks[13], (DEPTH, HEAD_DIM), f32) * 0.1
    lambda_k2 = nrm(ks[14], (DEPTH, HEAD_DIM), f32) * 0.1
    subln_g = 1.0 + 0.02 * nrm(ks[15], (DEPTH, 2 * HEAD_DIM), f32)
    ln1_g = 1.0 + 0.02 * nrm(ks[16], (DEPTH, D_MODEL), f32)
    ln1_b = 0.02 * nrm(ks[17], (DEPTH, D_MODEL), f32)
    ln2_g = 1.0 + 0.02 * nrm(ks[18], (DEPTH, D_MODEL), f32)
    ln2_b = 0.02 * nrm(ks[19], (DEPTH, D_MODEL), f32)
    w_gate_up = nrm(ks[20], (DEPTH, D_MODEL, 2 * D_FF), f32) * (s_in * BETA)
    w_down = nrm(ks[21], (DEPTH, D_FF, D_MODEL), f32) * (D_FF ** -0.5 * BETA)
    return {'x_prompt': x_prompt, 'x_sample': x_sample, 'cache_k': cache_k, 'cache_v': cache_v,
            'state_conv': state_conv, 'page_table': page_table, 'meta_tokens': meta_tokens,
            'w_in': w_in, 'conv_w': conv_w, 'w_out': w_out, 'lambda_q1': lambda_q1,
            'lambda_k1': lambda_k1, 'lambda_q2': lambda_q2, 'lambda_k2': lambda_k2,
            'subln_g': subln_g, 'ln1_g': ln1_g, 'ln1_b': ln1_b, 'ln2_g': ln2_g, 'ln2_b': ln2_b,
            'w_gate_up': w_gate_up, 'w_down': w_down}


def reference(x_prompt, x_sample, cache_k, cache_v, state_conv, page_table, meta_tokens, w_in, conv_w,
              w_out, lambda_q1, lambda_k1, lambda_q2, lambda_k2, subln_g, ln1_g, ln1_b, ln2_g, ln2_b,
              w_gate_up, w_down):
    params = (w_in, conv_w, w_out, lambda_q1, lambda_k1, lambda_q2, lambda_k2, subln_g,
              ln1_g, ln1_b, ln2_g, ln2_b, w_gate_up, w_down)

    bsz = x_prompt.shape[0]
    meta = jnp.broadcast_to(meta_tokens.astype(x_prompt.dtype)[None], (bsz, N_META, x_prompt.shape[-1]))
    xp = jnp.concatenate([meta, x_prompt], axis=1)
    pos_p = jnp.arange(xp.shape[1], dtype=jnp.int32)
    zero_prefix = jnp.zeros((bsz, CONV_W - 1, CONV_DIM), xp.dtype)
    kp, vp, cp = [], [], []
    for l in range(DEPTH):
        xp, k_l, v_l, c_l = _layer(xp, pos_p, zero_prefix, _prompt_attend, l, *params)
        kp.append(k_l)
        vp.append(v_l)
        cp.append(c_l)
    y_prompt = xp[:, N_META:]

    dbsz, dseq = x_sample.shape[0], x_sample.shape[1]
    past_len = page_table.shape[1] * cache_k.shape[2]
    pos_s = past_len + jnp.arange(dseq, dtype=jnp.int32)
    k_pos_s = jnp.arange(past_len + dseq, dtype=jnp.int32)

    def sample_attend(q, k, v, pos, lam, sub_g, lam_init, l):
        past_k = cache_k[l][page_table].reshape(dbsz, past_len, 2 * N_HEADS, HEAD_DIM).astype(k.dtype)
        past_v = cache_v[l][page_table].reshape(dbsz, past_len, N_HEADS, 2 * HEAD_DIM).astype(v.dtype)
        kk = jnp.concatenate([past_k, k], axis=1)
        vv = jnp.concatenate([past_v, v], axis=1)
        return _diff_attend(q, kk, vv, pos, k_pos_s, lam, sub_g, lam_init)

    xs = x_sample
    ks_, vs_, cs_ = [], [], []
    for l in range(DEPTH):
        xs, k_l, v_l, c_l = _layer(xs, pos_s, state_conv[l], sample_attend, l, *params)
        ks_.append(k_l)
        vs_.append(v_l)
        cs_.append(c_l)
    y_sample = xs

    new_k_prompt = jnp.stack(kp)
    new_v_prompt = jnp.stack(vp)
    new_conv_prompt = jnp.stack(cp)
    new_k_sample = jnp.stack(ks_)
    new_v_sample = jnp.stack(vs_)
    new_conv_sample = jnp.stack(cs_)
    return (y_prompt, y_sample, new_k_prompt, new_v_prompt, new_conv_prompt,
            new_k_sample, new_v_sample, new_conv_sample)
```

```python
import functools
import math

import jax
import jax.numpy as jnp
from jax import lax
from jax.experimental import pallas as pl
from jax.experimental.pallas import tpu as pltpu

F32 = jnp.float32
BF16 = jnp.bfloat16

N_META = 16
CONV_W = 3
N_HEADS = 4
N_SUB = 2 * N_HEADS
HEAD_DIM = 64
V_DIM = 2 * HEAD_DIM
ROPE_THETA = 10000.0
LN_EPS = 1e-5
NEG_INF = -1e30
LANES = 128
FF_CHUNK = 256
VMEM_LIMIT = 56 * 1024 * 1024


def _params(*sem):
    return pltpu.CompilerParams(dimension_semantics=sem, vmem_limit_bytes=VMEM_LIMIT)


def _full(shape):
    return pl.BlockSpec(shape, lambda *_: (0,) * len(shape))


def _rope128(x, cos, sin_signed):
    lane = lax.broadcasted_iota(jnp.int32, x.shape, 1)
    swapped = jnp.where((lane & (HEAD_DIM // 2)) == 0,
                        pltpu.roll(x, LANES - HEAD_DIM // 2, 1), pltpu.roll(x, HEAD_DIM // 2, 1))
    return x * cos + swapped * sin_signed


def _layernorm(h, g, b):
    mu = jnp.mean(h, axis=-1, keepdims=True)
    d = h - mu
    var = jnp.mean(d * d, axis=-1, keepdims=True)
    return d * lax.rsqrt(var + LN_EPS) * g + b


def _lam(lq1, lk1, lq2, lk2, lam_init):
    a = jnp.sum(lq1[...] * lk1[...], axis=1, keepdims=True)
    b = jnp.sum(lq2[...] * lk2[...], axis=1, keepdims=True)
    return jnp.exp(a) - jnp.exp(b) + lam_init


def _diff_norm(o1, o2, lam, g, lam_init):
    o = o1 - lam * o2
    o = o * lax.rsqrt(jnp.mean(o * o, axis=-1, keepdims=True) + LN_EPS)
    return o * g * (1.0 - lam_init)


def _lanes(x, width):
    reps = width // LANES
    return x if reps == 1 else jnp.concatenate([x] * reps, axis=1)


def _online_softmax(s, m_prev, l_prev):
    m_new = jnp.maximum(m_prev, jnp.max(s, axis=1, keepdims=True))
    alpha = jnp.exp(m_prev - m_new)
    p = jnp.exp(s - _lanes(m_new, s.shape[1]))
    l_new = alpha * l_prev + jnp.sum(p, axis=1, keepdims=True)
    return p, alpha, m_new, l_new


def _dot_nt(a, b):
    return lax.dot_general(a, b, (((1,), (1,)), ((), ())), preferred_element_type=F32)


def _project(x_ref, w_ref, conv_dim):
    xb = x_ref[...].astype(BF16)

    def proj(seg):
        return jnp.dot(xb, w_ref[:, seg * conv_dim:(seg + 1) * conv_dim], preferred_element_type=F32)

    return proj


def _rope_store(val, cos, sin, scale, refs):
    for c in range(val.shape[1] // LANES):
        sl = slice(c * LANES, (c + 1) * LANES)
        r = _rope128(val[:, sl], cos, sin)
        if scale != 1.0:
            r = r * scale
        for ref in refs:
            ref[:, sl] = r.astype(ref.dtype)


def _inproj_main_kernel(x_ref, w_ref, cw_ref, cos_ref, sin_ref, upre_ref,
                        yc_ref, q_ref, k_ref, v_ref, kb_ref, vb_ref, cs_ref, ubuf,
                        *, tm, tiles_per_seq, conv_dim):
    i = pl.program_id(0)
    proj = _project(x_ref, w_ref, conv_dim)
    u = proj(2) * proj(0)

    @pl.when(i % tiles_per_seq == 0)
    def _():
        ubuf[6:8, :] = upre_ref[...]

    ubuf[8:8 + tm, :] = u
    cw = cw_ref[...]
    conv = ubuf[6:6 + tm, :] * cw[0:1] + ubuf[7:7 + tm, :] * cw[1:2]
    conv = conv + ubuf[8:8 + tm, :] * cw[2:3]
    yc_ref[...] = (proj(1) * conv).astype(yc_ref.dtype)
    tail = ubuf[tm + 6:tm + 8, :]
    cs_ref[...] = tail
    ubuf[6:8, :] = tail

    cos = cos_ref[...]
    sin = sin_ref[...]
    _rope_store(proj(3), cos, sin, HEAD_DIM ** -0.5, (q_ref,))
    _rope_store(proj(4), cos, sin, 1.0, (k_ref, kb_ref))
    vv = proj(5)
    v_ref[...] = vv
    vb_ref[...] = vv.astype(vb_ref.dtype)


def _inproj_side_kernel(x_ref, w_ref, cw_ref, cos_ref, sin_ref, pre1_ref, pre2_ref, sel1_ref, sel2_ref,
                        yc_ref, q_ref, k_ref, v_ref, u_ref, ubuf, *, tm, conv_dim):
    proj = _project(x_ref, w_ref, conv_dim)
    u = proj(2) * proj(0)
    u_ref[...] = u
    ubuf[0:8, :] = jnp.zeros((8, conv_dim), F32)
    ubuf[8:8 + tm, :] = u
    um2 = jnp.where(sel2_ref[...] > 0.5, pre2_ref[...], ubuf[6:6 + tm, :])
    um1 = jnp.where(sel1_ref[...] > 0.5, pre1_ref[...], ubuf[7:7 + tm, :])
    cw = cw_ref[...]
    conv = um2 * cw[0:1] + um1 * cw[1:2]
    conv = conv + u * cw[2:3]
    yc_ref[...] = proj(1) * conv
    cos = cos_ref[...]
    sin = sin_ref[...]
    _rope_store(proj(3), cos, sin, HEAD_DIM ** -0.5, (q_ref,))
    _rope_store(proj(4), cos, sin, 1.0, (k_ref,))
    v_ref[...] = proj(5)


def _inproj_main(x, w, cw, cos, sin, upre, *, seq, tm):
    rows, d_model = x.shape
    conv_dim = cw.shape[1]
    attn_dim = conv_dim
    tiles_per_seq = seq // tm
    n_seq = rows // seq
    row_blk = lambda width: pl.BlockSpec((tm, width), lambda i: (i, 0))
    tab_blk = pl.BlockSpec((tm, LANES), lambda i: (i % tiles_per_seq, 0))
    out_shape = (
        jax.ShapeDtypeStruct((rows, conv_dim), BF16),
        jax.ShapeDtypeStruct((rows, attn_dim), BF16),
        jax.ShapeDtypeStruct((rows, attn_dim), F32),
        jax.ShapeDtypeStruct((rows, attn_dim), F32),
        jax.ShapeDtypeStruct((rows, attn_dim), BF16),
        jax.ShapeDtypeStruct((rows, attn_dim), BF16),
        jax.ShapeDtypeStruct((n_seq, CONV_W - 1, conv_dim), F32),
    )
    return pl.pallas_call(
        functools.partial(_inproj_main_kernel, tm=tm, tiles_per_seq=tiles_per_seq, conv_dim=conv_dim),
        grid=(rows // tm,),
        in_specs=[row_blk(d_model), _full(w.shape), _full(cw.shape), tab_blk, tab_blk, _full(upre.shape)],
        out_specs=(row_blk(conv_dim), row_blk(attn_dim), row_blk(attn_dim), row_blk(attn_dim),
                   row_blk(attn_dim), row_blk(attn_dim),
                   pl.BlockSpec((None, CONV_W - 1, conv_dim), lambda i: (i // tiles_per_seq, 0, 0))),
        out_shape=out_shape,
        scratch_shapes=[pltpu.VMEM((tm + 8, conv_dim), F32)],
        compiler_params=_params("arbitrary"),
        name="inproj_main",
    )(x, w, cw, cos, sin, upre)


def _inproj_side(x, w, cw, cos, sin, pre1, pre2, sel1, sel2):
    rows, _ = x.shape
    conv_dim = cw.shape[1]
    args = (x, w, cw, cos, sin, pre1, pre2, sel1, sel2)
    out = jax.ShapeDtypeStruct((rows, conv_dim), F32)
    return pl.pallas_call(
        functools.partial(_inproj_side_kernel, tm=rows, conv_dim=conv_dim),
        grid=(1,),
        in_specs=[_full(a.shape) for a in args],
        out_specs=tuple(_full(out.shape) for _ in range(5)),
        out_shape=(out,) * 5,
        scratch_shapes=[pltpu.VMEM((rows + 8, conv_dim), F32)],
        compiler_params=_params("arbitrary"),
        name="inproj_side",
    )(*args)


def _prompt_attn_kernel(qi_ref, kj_ref, q_ref, k_ref, v_ref, km_ref, vm_ref,
                        lq1, lk1, lq2, lk2, sg_ref, o_ref, m_sc, l_sc, acc_sc, *, tq, tk, lam_init):
    step = pl.program_id(1)
    i = qi_ref[step]
    j = kj_ref[step]

    def head_slices(h):
        return slice(h * HEAD_DIM, (h + 1) * HEAD_DIM), slice((h // 2) * V_DIM, (h // 2 + 1) * V_DIM)

    @pl.when(j == 0)
    def _():
        col = lax.broadcasted_iota(jnp.int32, (tq, LANES), 1)
        for h in range(N_SUB):
            qk, vv = head_slices(h)
            s = jnp.where(col < N_META, _dot_nt(q_ref[:, qk], km_ref[:, qk]), NEG_INF)
            m = jnp.max(s, axis=1, keepdims=True)
            p = jnp.exp(s - m)
            m_sc[h] = jnp.broadcast_to(m, (tq, LANES))
            l_sc[h] = jnp.broadcast_to(jnp.sum(p, axis=1, keepdims=True), (tq, LANES))
            acc_sc[h] = jnp.dot(p.astype(BF16), vm_ref[:, vv], preferred_element_type=F32)

    def process(masked):
        if masked:
            row = lax.broadcasted_iota(jnp.int32, (tq, tk), 0)
            col = lax.broadcasted_iota(jnp.int32, (tq, tk), 1)
            visible = col <= row
        for h in range(N_SUB):
            qk, vv = head_slices(h)
            s = _dot_nt(q_ref[:, qk], k_ref[:, qk])
            if masked:
                s = jnp.where(visible, s, NEG_INF)
            p, alpha, m_new, l_new = _online_softmax(s, m_sc[h], l_sc[h])
            m_sc[h] = m_new
            l_sc[h] = l_new
            acc_sc[h] = acc_sc[h] * alpha + jnp.dot(p.astype(BF16), v_ref[:, vv], preferred_element_type=F32)

    @pl.when(j < i)
    def _():
        process(False)

    @pl.when(j == i)
    def _():
        process(True)
        lam = _lam(lq1, lk1, lq2, lk2, lam_init)
        g = sg_ref[...]
        for hv in range(N_HEADS):
            o1 = acc_sc[2 * hv] / l_sc[2 * hv]
            o2 = acc_sc[2 * hv + 1] / l_sc[2 * hv + 1]
            o_ref[:, hv * V_DIM:(hv + 1) * V_DIM] = _diff_norm(o1, o2, lam, g, lam_init).astype(o_ref.dtype)


def _prompt_attention(q, k, v, k_meta, v_meta, lam_vecs, sub_g, *, seq, tq, lam_init):
    rows, attn_dim = q.shape
    n_seq = rows // seq
    nq = seq // tq
    pairs = [(i, j) for i in range(nq) for j in range(i + 1)]
    qi = jnp.asarray([p[0] for p in pairs], jnp.int32)
    kj = jnp.asarray([p[1] for p in pairs], jnp.int32)
    q_blk = pl.BlockSpec((tq, attn_dim), lambda b, s, qi, kj: (b * nq + qi[s], 0))
    kv_blk = pl.BlockSpec((tq, attn_dim), lambda b, s, qi, kj: (b * nq + kj[s], 0))
    small = [_full(a.shape) for a in (k_meta, v_meta, *lam_vecs, sub_g)]
    grid_spec = pltpu.PrefetchScalarGridSpec(
        num_scalar_prefetch=2,
        grid=(n_seq, len(pairs)),
        in_specs=[q_blk, kv_blk, kv_blk] + small,
        out_specs=q_blk,
        scratch_shapes=[pltpu.VMEM((N_SUB, tq, LANES), F32), pltpu.VMEM((N_SUB, tq, LANES), F32),
                        pltpu.VMEM((N_SUB, tq, V_DIM), F32)],
    )
    return pl.pallas_call(
        functools.partial(_prompt_attn_kernel, tq=tq, tk=tq, lam_init=lam_init),
        grid_spec=grid_spec,
        out_shape=jax.ShapeDtypeStruct((rows, attn_dim), BF16),
        compiler_params=_params("parallel", "arbitrary"),
        name="prompt_attn",
    )(qi, kj, q, k, v, k_meta, v_meta, *lam_vecs, sub_g)


def _side_attn_kernel(pt_ref, q_ref, kn_ref, vn_ref, *rest, n, pages_per_step, lam_init):
    del pt_ref
    k_refs = rest[:pages_per_step]
    v_refs = rest[pages_per_step:2 * pages_per_step]
    lq1, lk1, lq2, lk2, sg_ref, o_ref, qbd_sc, m_sc, l_sc, acc_sc = rest[2 * pages_per_step:]
    g = pl.program_id(1)
    rows = N_SUB * n
    attn_dim = q_ref.shape[1]

    @pl.when(g == 0)
    def _():
        qt = jnp.concatenate([q_ref[...]] * N_SUB, axis=0)
        row = lax.broadcasted_iota(jnp.int32, (rows, attn_dim), 0)
        lane = lax.broadcasted_iota(jnp.int32, (rows, attn_dim), 1)
        own = (row // n) == (lane // HEAD_DIM)
        qbd_sc[...] = jnp.where(own, qt, 0.0).astype(qbd_sc.dtype)
        m_sc[...] = jnp.full(m_sc.shape, NEG_INF, F32)
        l_sc[...] = jnp.zeros(l_sc.shape, F32)
        acc_sc[...] = jnp.zeros(acc_sc.shape, F32)

    def update(s, values):
        p, alpha, m_new, l_new = _online_softmax(s, m_sc[...], l_sc[...])
        m_sc[...] = m_new
        l_sc[...] = l_new
        acc = acc_sc[...] * _lanes(alpha, attn_dim)
        for c, val in enumerate(values):
            acc = acc + jnp.dot(p[:, c * LANES:(c + 1) * LANES].astype(BF16), val.astype(BF16),
                                preferred_element_type=F32)
        acc_sc[...] = acc

    if pages_per_step:
        qbd = qbd_sc[...]
        s = jnp.concatenate([_dot_nt(qbd, k_ref[...].astype(BF16)) for k_ref in k_refs], axis=1)
        update(s, [v_ref[...] for v_ref in v_refs])

    @pl.when(g == pl.num_programs(1) - 1)
    def _():
        pad = jnp.zeros((LANES - n, attn_dim), F32)
        kn = jnp.concatenate([kn_ref[...], pad], axis=0).astype(BF16)
        vn = jnp.concatenate([vn_ref[...], pad], axis=0)
        row = lax.broadcasted_iota(jnp.int32, (rows, LANES), 0)
        col = lax.broadcasted_iota(jnp.int32, (rows, LANES), 1)
        s = jnp.where(col <= (row % n), _dot_nt(qbd_sc[...], kn), NEG_INF)
        update(s, [vn])
        lam = _lam(lq1, lk1, lq2, lk2, lam_init)
        gain = sg_ref[...]
        inv_l = 1.0 / l_sc[...]
        for hv in range(N_HEADS):
            r0 = 2 * hv * n
            lanes = slice(hv * V_DIM, (hv + 1) * V_DIM)
            o1 = acc_sc[r0:r0 + n, lanes] * inv_l[r0:r0 + n]
            o2 = acc_sc[r0 + n:r0 + 2 * n, lanes] * inv_l[r0 + n:r0 + 2 * n]
            o_ref[:, lanes] = _diff_norm(o1, o2, lam, gain, lam_init)


def _side_attention(q, k_new, v_new, lam_vecs, sub_g, *, n, lam_init, cache=None, pages_per_step=0):
    rows, attn_dim = q.shape
    n_seq = rows // n
    tok_blk = pl.BlockSpec((n, attn_dim), lambda b, g, pt: (b, 0))
    small = [_full(a.shape) for a in (*lam_vecs, sub_g)]
    if cache is None:
        page_table = jnp.zeros((1,), jnp.int32)
        steps, kv_specs, kv_args = 1, [], []
    else:
        cache_k, cache_v, layer, page_table = cache
        n_pages = page_table.shape[1]
        page_size = cache_k.shape[2]
        steps = n_pages // pages_per_step
        page_table = page_table.reshape(-1)

        def page_blk(jj):
            return pl.BlockSpec(
                (None, None, page_size, attn_dim),
                lambda b, g, pt: (layer, pt[b * n_pages + g * pages_per_step + jj], 0, 0))

        kv_specs = [page_blk(jj) for jj in range(pages_per_step)] * 2
        kv_args = [cache_k] * pages_per_step + [cache_v] * pages_per_step
    grid_spec = pltpu.PrefetchScalarGridSpec(
        num_scalar_prefetch=1,
        grid=(n_seq, steps),
        in_specs=[tok_blk, tok_blk, tok_blk] + kv_specs + small,
        out_specs=tok_blk,
        scratch_shapes=[pltpu.VMEM((N_SUB * n, attn_dim), BF16), pltpu.VMEM((N_SUB * n, LANES), F32),
                        pltpu.VMEM((N_SUB * n, LANES), F32), pltpu.VMEM((N_SUB * n, attn_dim), F32)],
    )
    return pl.pallas_call(
        functools.partial(_side_attn_kernel, n=n, pages_per_step=pages_per_step if cache else 0,
                          lam_init=lam_init),
        grid_spec=grid_spec,
        out_shape=jax.ShapeDtypeStruct((rows, attn_dim), F32),
        compiler_params=_params("parallel", "arbitrary"),
        name="sample_attn" if cache else "meta_attn",
    )(page_table, q, k_new, v_new, *kv_args, *lam_vecs, sub_g)


def _post_kernel(yc_ref, ya_ref, x_ref, wo_ref, wg_ref, wu_ref, wd_ref, g1, b1, g2, b2,
                 o_ref, x1_sc, x1b_sc, acc_sc, *, alpha, conv_dim):
    mix = jnp.dot(yc_ref[...].astype(BF16), wo_ref[0:conv_dim, :], preferred_element_type=F32)
    mix = mix + jnp.dot(ya_ref[...].astype(BF16), wo_ref[conv_dim:, :], preferred_element_type=F32)
    x1 = _layernorm(alpha * x_ref[...] + mix, g1[...], b1[...])
    x1_sc[...] = x1
    x1b_sc[...] = x1.astype(BF16)
    acc_sc[...] = jnp.zeros(acc_sc.shape, F32)

    def chunk(c, carry):
        x1b = x1b_sc[...]
        gate = jnp.dot(x1b, wg_ref[c], preferred_element_type=F32)
        up = jnp.dot(x1b, wu_ref[c], preferred_element_type=F32)
        hidden = (gate / (1.0 + jnp.exp(-gate))) * up
        acc_sc[...] += jnp.dot(hidden.astype(BF16), wd_ref[c], preferred_element_type=F32)
        return carry

    lax.fori_loop(0, wg_ref.shape[0], chunk, 0)
    o_ref[...] = _layernorm(alpha * x1_sc[...] + acc_sc[...], g2[...], b2[...])


def _post_attention(yc, ya, x, wo, wg, wu, wd, g1, b1, g2, b2, *, tm, alpha):
    rows, d_model = x.shape
    conv_dim = yc.shape[1]
    row_blk = lambda width: pl.BlockSpec((tm, width), lambda i: (i, 0))
    consts = (wo, wg, wu, wd, g1, b1, g2, b2)
    return pl.pallas_call(
        functools.partial(_post_kernel, alpha=alpha, conv_dim=conv_dim),
        grid=(rows // tm,),
        in_specs=[row_blk(conv_dim), row_blk(ya.shape[1]), row_blk(d_model)] + [_full(a.shape) for a in consts],
        out_specs=row_blk(d_model),
        out_shape=jax.ShapeDtypeStruct((rows, d_model), F32),
        scratch_shapes=[pltpu.VMEM((tm, d_model), F32), pltpu.VMEM((tm, d_model), BF16),
                        pltpu.VMEM((tm, d_model), F32)],
        compiler_params=_params("parallel"),
        name="post_main" if rows > tm else "post_side",
    )(yc, ya, x, *consts)


def _rope_tables(pos):
    inv = 1.0 / (ROPE_THETA ** (jnp.arange(0, HEAD_DIM, 2, dtype=F32) / HEAD_DIM))
    ang = pos.astype(F32)[:, None] * inv[None, :]
    cos = jnp.cos(ang)
    sin = jnp.sin(ang)
    return jnp.concatenate([cos] * 4, axis=-1), jnp.concatenate([-sin, sin] * 2, axis=-1)


def _row_tile(n, target):
    t = min(n, target)
    while n % t:
        t //= 2
    return t


def kernel(x_prompt, x_sample, cache_k, cache_v, state_conv, page_table, meta_tokens, w_in, conv_w, w_out, lambda_q1, lambda_k1, lambda_q2, lambda_k2, subln_g, ln1_g, ln1_b, ln2_g, ln2_b, w_gate_up, w_down):
    batch, seq, d_model = x_prompt.shape
    dec_batch, dec_seq, _ = x_sample.shape
    depth = w_in.shape[0]
    conv_dim = conv_w.shape[2]
    attn_dim = N_SUB * HEAD_DIM
    d_ff = w_down.shape[1]
    n_pool, page_size = cache_k.shape[1], cache_k.shape[2]
    past_len = page_table.shape[1] * page_size
    alpha = (2 * depth) ** 0.25
    n_sample = dec_batch * dec_seq
    n_side = n_sample + N_META
    n_chunks = d_ff // FF_CHUNK
    tm = _row_tile(seq, 512)
    pages_per_step = _row_tile(page_table.shape[1], 8)

    w_in_b = w_in.astype(BF16)
    w_out_b = w_out.astype(BF16)
    w_gate = w_gate_up[:, :, :d_ff].reshape(depth, d_model, n_chunks, FF_CHUNK).transpose(0, 2, 1, 3).astype(BF16)
    w_up = w_gate_up[:, :, d_ff:].reshape(depth, d_model, n_chunks, FF_CHUNK).transpose(0, 2, 1, 3).astype(BF16)
    w_down_b = w_down.reshape(depth, n_chunks, FF_CHUNK, d_model).astype(BF16)
    cache_k2 = cache_k.reshape(depth, n_pool, page_size, attn_dim)
    cache_v2 = cache_v.reshape(depth, n_pool, page_size, attn_dim)

    cos_m, sin_m = _rope_tables(N_META + jnp.arange(seq, dtype=jnp.int32))
    pos_side = jnp.concatenate([jnp.tile(past_len + jnp.arange(dec_seq, dtype=jnp.int32), dec_batch),
                                jnp.arange(N_META, dtype=jnp.int32)])
    cos_s, sin_s = _rope_tables(pos_side)

    j_in_seq = jnp.concatenate([jnp.tile(jnp.arange(dec_seq, dtype=jnp.int32), dec_batch),
                                jnp.arange(N_META, dtype=jnp.int32)])
    sel1 = (j_in_seq < 1).astype(F32)[:, None]
    sel2 = (j_in_seq < 2).astype(F32)[:, None]

    def prefixes(state):
        z = jnp.zeros((dec_batch, dec_seq, conv_dim), F32)
        p1 = z.at[:, 0].set(state[:, 1]).reshape(n_sample, conv_dim)
        p2 = z.at[:, 0].set(state[:, 0]).at[:, 1].set(state[:, 1]).reshape(n_sample, conv_dim)
        zm = jnp.zeros((N_META, conv_dim), F32)
        return jnp.concatenate([p1, zm]), jnp.concatenate([p2, zm])

    x_main = x_prompt.reshape(batch * seq, d_model)
    x_side = jnp.concatenate([x_sample.reshape(n_sample, d_model), meta_tokens.astype(F32)])

    outs = {name: [] for name in ("kp", "vp", "cp", "ks", "vs", "cs")}
    for l in range(depth):
        lam_init = 0.8 - 0.6 * math.exp(-0.3 * l)
        lam_vecs = tuple(a[l][None, :] for a in (lambda_q1, lambda_k1, lambda_q2, lambda_k2))
        sub_g = subln_g[l][None, :]
        row = lambda a: a[l][None, :]

        pre1, pre2 = prefixes(state_conv[l])
        yc_s, q_s, k_s, v_s, u_s = _inproj_side(x_side, w_in_b[l], conv_w[l], cos_s, sin_s, pre1, pre2, sel1, sel2)
        k_meta, v_meta, u_meta = k_s[n_sample:], v_s[n_sample:], u_s[n_sample:]

        yc_m, q_m, k_m, v_m, kb_m, vb_m, cstate = _inproj_main(
            x_main, w_in_b[l], conv_w[l], cos_m, sin_m, u_meta[N_META - (CONV_W - 1):], seq=seq, tm=tm)

        pad = jnp.zeros((LANES - N_META, attn_dim), BF16)
        ya_m = _prompt_attention(q_m, kb_m, vb_m,
                                 jnp.concatenate([k_meta.astype(BF16), pad]),
                                 jnp.concatenate([v_meta.astype(BF16), pad]),
                                 lam_vecs, sub_g, seq=seq, tq=tm, lam_init=lam_init)
        ya_sample = _side_attention(q_s[:n_sample], k_s[:n_sample], v_s[:n_sample], lam_vecs, sub_g,
                                    n=dec_seq, lam_init=lam_init,
                                    cache=(cache_k2, cache_v2, l, page_table), pages_per_step=pages_per_step)
        ya_meta = _side_attention(q_s[n_sample:], k_meta, v_meta, lam_vecs, sub_g, n=N_META, lam_init=lam_init)
        ya_s = jnp.concatenate([ya_sample, ya_meta])

        post = functools.partial(_post_attention, wo=w_out_b[l], wg=w_gate[l], wu=w_up[l], wd=w_down_b[l],
                                 g1=row(ln1_g), b1=row(ln1_b), g2=row(ln2_g), b2=row(ln2_b), alpha=alpha)
        x_main = post(yc_m, ya_m, x_main, tm=tm)
        x_side = post(yc_s, ya_s, x_side, tm=n_side)

        def with_meta(meta_rows, real, shape):
            meta_b = jnp.broadcast_to(meta_rows[None], (batch,) + meta_rows.shape)
            return jnp.concatenate([meta_b, real.reshape(batch, seq, -1)], axis=1).reshape(shape)

        outs["kp"].append(with_meta(k_meta, k_m, (batch, N_META + seq, N_SUB, HEAD_DIM)))
        outs["vp"].append(with_meta(v_meta, v_m, (batch, N_META + seq, N_HEADS, V_DIM)))
        outs["cp"].append(cstate)
        outs["ks"].append(k_s[:n_sample].reshape(dec_batch, dec_seq, N_SUB, HEAD_DIM))
        outs["vs"].append(v_s[:n_sample].reshape(dec_batch, dec_seq, N_HEADS, V_DIM))
        outs["cs"].append(u_s[:n_sample].reshape(dec_batch, dec_seq, conv_dim)[:, dec_seq - (CONV_W - 1):])

    y_prompt = x_main.reshape(batch, seq, d_model)
    y_sample = x_side[:n_sample].reshape(dec_batch, dec_seq, d_model)
    return (y_prompt, y_sample, jnp.stack(outs["kp"]), jnp.stack(outs["vp"]), jnp.stack(outs["cp"]),
            jnp.stack(outs["ks"]), jnp.stack(outs["vs"]), jnp.stack(outs["cs"]))
```

```python
import functools
import math

import jax
import jax.numpy as jnp
from jax import lax
from jax.experimental import pallas as pl
from jax.experimental.pallas import tpu as pltpu

F32 = jnp.float32
BF16 = jnp.bfloat16

N_META = 16
CONV_W = 3
N_HEADS = 4
N_SUB = 2 * N_HEADS
HEAD_DIM = 64
V_DIM = 2 * HEAD_DIM
ROPE_THETA = 10000.0
LN_EPS = 1e-5
NEG_INF = -1e30
LANES = 128
FF_CHUNK = 256
VMEM_LIMIT = 56 * 1024 * 1024


def _params(*sem):
    return pltpu.CompilerParams(dimension_semantics=sem, vmem_limit_bytes=VMEM_LIMIT)


def _full(shape):
    return pl.BlockSpec(shape, lambda *_: (0,) * len(shape))


def _rope128(x, cos, sin_signed):
    lane = lax.broadcasted_iota(jnp.int32, x.shape, 1)
    swapped = jnp.where((lane & (HEAD_DIM // 2)) == 0,
                        pltpu.roll(x, LANES - HEAD_DIM // 2, 1), pltpu.roll(x, HEAD_DIM // 2, 1))
    return x * cos + swapped * sin_signed


def _layernorm(h, g, b):
    mu = jnp.mean(h, axis=-1, keepdims=True)
    d = h - mu
    var = jnp.mean(d * d, axis=-1, keepdims=True)
    return d * lax.rsqrt(var + LN_EPS) * g + b


def _lam(lq1, lk1, lq2, lk2, lam_init):
    a = jnp.sum(lq1[...] * lk1[...], axis=1, keepdims=True)
    b = jnp.sum(lq2[...] * lk2[...], axis=1, keepdims=True)
    return jnp.exp(a) - jnp.exp(b) + lam_init


def _diff_norm(o1, o2, lam, g, lam_init):
    o = o1 - lam * o2
    o = o * lax.rsqrt(jnp.mean(o * o, axis=-1, keepdims=True) + LN_EPS)
    return o * g * (1.0 - lam_init)


def _lanes(x, width):
    reps = width // LANES
    return x if reps == 1 else jnp.concatenate([x] * reps, axis=1)


def _online_softmax(s, m_prev, l_prev):
    m_new = jnp.maximum(m_prev, jnp.max(s, axis=1, keepdims=True))
    alpha = jnp.exp(m_prev - m_new)
    p = jnp.exp(s - _lanes(m_new, s.shape[1]))
    l_new = alpha * l_prev + jnp.sum(p, axis=1, keepdims=True)
    return p, alpha, m_new, l_new


def _dot_nt(a, b):
    return lax.dot_general(a, b, (((1,), (1,)), ((), ())), preferred_element_type=F32)


def _project(x_ref, w_ref, conv_dim):
    xb = x_ref[...].astype(BF16)

    def proj(seg):
        return jnp.dot(xb, w_ref[:, seg * conv_dim:(seg + 1) * conv_dim], preferred_element_type=F32)

    return proj


def _rope_store(val, cos, sin, scale, refs):
    for c in range(val.shape[1] // LANES):
        sl = slice(c * LANES, (c + 1) * LANES)
        r = _rope128(val[:, sl], cos, sin)
        if scale != 1.0:
            r = r * scale
        for ref in refs:
            ref[:, sl] = r.astype(ref.dtype)


def _inproj_main_kernel(x_ref, w_ref, wkt_ref, cw_ref, cos_ref, sin_ref, cost_ref, sint_ref, upre_ref,
                        yc_ref, q_ref, kt_ref, v_ref, ktb_ref, vb_ref, cs_ref, ubuf,
                        *, tm, tiles_per_seq, conv_dim):
    i = pl.program_id(0)
    proj = _project(x_ref, w_ref, conv_dim)
    u = proj(2) * proj(0)

    @pl.when(i % tiles_per_seq == 0)
    def _():
        ubuf[6:8, :] = upre_ref[...]

    ubuf[8:8 + tm, :] = u
    cw = cw_ref[...]
    conv = ubuf[6:6 + tm, :] * cw[0:1] + ubuf[7:7 + tm, :] * cw[1:2]
    conv = conv + ubuf[8:8 + tm, :] * cw[2:3]
    yc_ref[...] = (proj(1) * conv).astype(yc_ref.dtype)
    tail = ubuf[tm + 6:tm + 8, :]
    cs_ref[...] = tail
    ubuf[6:8, :] = tail

    cos = cos_ref[...]
    sin = sin_ref[...]
    _rope_store(proj(3), cos, sin, HEAD_DIM ** -0.5, (q_ref,))

    kt = _dot_nt(wkt_ref[...], x_ref[...].astype(BF16))
    cost = cost_ref[...]
    sint = sint_ref[...]
    half = HEAD_DIM // 2
    for h in range(N_SUB):
        blk = kt[h * HEAD_DIM:(h + 1) * HEAD_DIM, :]
        swapped = jnp.concatenate([blk[half:], blk[:half]], axis=0)
        r = blk * cost + swapped * sint
        kt_ref[h * HEAD_DIM:(h + 1) * HEAD_DIM, :] = r
        ktb_ref[h * HEAD_DIM:(h + 1) * HEAD_DIM, :] = r.astype(ktb_ref.dtype)

    vv = proj(4)
    vb_ref[...] = vv.astype(vb_ref.dtype)
    for hv in range(N_HEADS):
        v_ref[pl.ds(hv, tm, stride=N_HEADS), :] = vv[:, hv * V_DIM:(hv + 1) * V_DIM]


def _inproj_side_kernel(x_ref, w_ref, cw_ref, cos_ref, sin_ref, pre1_ref, pre2_ref, sel1_ref, sel2_ref,
                        yc_ref, q_ref, k_ref, v_ref, u_ref, ubuf, *, tm, conv_dim):
    proj = _project(x_ref, w_ref, conv_dim)
    u = proj(2) * proj(0)
    u_ref[...] = u
    ubuf[0:8, :] = jnp.zeros((8, conv_dim), F32)
    ubuf[8:8 + tm, :] = u
    um2 = jnp.where(sel2_ref[...] > 0.5, pre2_ref[...], ubuf[6:6 + tm, :])
    um1 = jnp.where(sel1_ref[...] > 0.5, pre1_ref[...], ubuf[7:7 + tm, :])
    cw = cw_ref[...]
    conv = um2 * cw[0:1] + um1 * cw[1:2]
    conv = conv + u * cw[2:3]
    yc_ref[...] = proj(1) * conv
    cos = cos_ref[...]
    sin = sin_ref[...]
    _rope_store(proj(3), cos, sin, HEAD_DIM ** -0.5, (q_ref,))
    _rope_store(proj(4), cos, sin, 1.0, (k_ref,))
    v_ref[...] = proj(5)


def _inproj_main(x, w, wkt, cw, cos, sin, cost, sint, upre, *, seq, tm):
    rows, d_model = x.shape
    conv_dim = cw.shape[1]
    attn_dim = wkt.shape[0]
    tiles_per_seq = seq // tm
    n_seq = rows // seq
    row_blk = lambda width: pl.BlockSpec((tm, width), lambda i: (i, 0))
    tab_blk = pl.BlockSpec((tm, LANES), lambda i: (i % tiles_per_seq, 0))
    tabt_blk = pl.BlockSpec((HEAD_DIM, tm), lambda i: (0, i % tiles_per_seq))
    kt_blk = pl.BlockSpec((None, attn_dim, tm), lambda i: (i // tiles_per_seq, 0, i % tiles_per_seq))
    out_shape = (
        jax.ShapeDtypeStruct((rows, conv_dim), BF16),
        jax.ShapeDtypeStruct((rows, attn_dim), BF16),
        jax.ShapeDtypeStruct((n_seq, attn_dim, seq), F32),
        jax.ShapeDtypeStruct((rows * N_HEADS, V_DIM), F32),
        jax.ShapeDtypeStruct((n_seq, attn_dim, seq), BF16),
        jax.ShapeDtypeStruct((rows, attn_dim), BF16),
        jax.ShapeDtypeStruct((n_seq, CONV_W - 1, conv_dim), F32),
    )
    return pl.pallas_call(
        functools.partial(_inproj_main_kernel, tm=tm, tiles_per_seq=tiles_per_seq, conv_dim=conv_dim),
        grid=(rows // tm,),
        in_specs=[row_blk(d_model), _full(w.shape), _full(wkt.shape), _full(cw.shape), tab_blk, tab_blk,
                  tabt_blk, tabt_blk, _full(upre.shape)],
        out_specs=(row_blk(conv_dim), row_blk(attn_dim), kt_blk,
                   pl.BlockSpec((tm * N_HEADS, V_DIM), lambda i: (i, 0)), kt_blk, row_blk(attn_dim),
                   pl.BlockSpec((None, CONV_W - 1, conv_dim), lambda i: (i // tiles_per_seq, 0, 0))),
        out_shape=out_shape,
        scratch_shapes=[pltpu.VMEM((tm + 8, conv_dim), F32)],
        compiler_params=_params("arbitrary"),
        name="inproj_main",
    )(x, w, wkt, cw, cos, sin, cost, sint, upre)


def _inproj_side(x, w, cw, cos, sin, pre1, pre2, sel1, sel2):
    rows, _ = x.shape
    conv_dim = cw.shape[1]
    args = (x, w, cw, cos, sin, pre1, pre2, sel1, sel2)
    out = jax.ShapeDtypeStruct((rows, conv_dim), F32)
    return pl.pallas_call(
        functools.partial(_inproj_side_kernel, tm=rows, conv_dim=conv_dim),
        grid=(1,),
        in_specs=[_full(a.shape) for a in args],
        out_specs=tuple(_full(out.shape) for _ in range(5)),
        out_shape=(out,) * 5,
        scratch_shapes=[pltpu.VMEM((rows + 8, conv_dim), F32)],
        compiler_params=_params("arbitrary"),
        name="inproj_side",
    )(*args)


def _prompt_attn_kernel(qi_ref, kj_ref, q_ref, kt_ref, v_ref, kmt_ref, vm_ref,
                        lq1, lk1, lq2, lk2, sg_ref, o_ref, m_sc, acc_sc, *, tq, tk, lam_init):
    step = pl.program_id(1)
    i = qi_ref[step]
    j = kj_ref[step]

    def head_slices(h):
        return slice(h * HEAD_DIM, (h + 1) * HEAD_DIM), slice((h // 2) * V_DIM, (h // 2 + 1) * V_DIM)

    def pv(p, v_blk):
        ones = jnp.ones((v_blk.shape[0], LANES), BF16)
        return jnp.dot(p.astype(BF16), jnp.concatenate([v_blk, ones], axis=1), preferred_element_type=F32)

    @pl.when(j == 0)
    def _():
        col = lax.broadcasted_iota(jnp.int32, (tq, LANES), 1)
        for h in range(N_SUB):
            qk, vv = head_slices(h)
            s = jnp.dot(q_ref[:, qk], kmt_ref[qk, :], preferred_element_type=F32)
            s = jnp.where(col < N_META, s, NEG_INF)
            m = jnp.max(s, axis=1, keepdims=True)
            m_sc[h] = jnp.broadcast_to(m, (tq, LANES))
            acc_sc[h] = pv(jnp.exp(s - m), vm_ref[:, vv])

    def process(masked):
        if masked:
            row = lax.broadcasted_iota(jnp.int32, (tq, tk), 0)
            col = lax.broadcasted_iota(jnp.int32, (tq, tk), 1)
            visible = col <= row
        for h in range(N_SUB):
            qk, vv = head_slices(h)
            s = jnp.dot(q_ref[:, qk], kt_ref[qk, :], preferred_element_type=F32)
            if masked:
                s = jnp.where(visible, s, NEG_INF)
            m_prev = m_sc[h]
            m_new = jnp.maximum(m_prev, jnp.max(s, axis=1, keepdims=True))
            alpha = jnp.exp(m_prev - m_new)
            p = jnp.exp(s - _lanes(m_new, tk))
            m_sc[h] = m_new
            acc_sc[h] = acc_sc[h] * _lanes(alpha, V_DIM + LANES) + pv(p, v_ref[:, vv])

    @pl.when(j < i)
    def _():
        process(False)

    @pl.when(j == i)
    def _():
        process(True)
        lam = _lam(lq1, lk1, lq2, lk2, lam_init)
        g = sg_ref[...]
        for hv in range(N_HEADS):
            a1 = acc_sc[2 * hv]
            a2 = acc_sc[2 * hv + 1]
            o1 = a1[:, :V_DIM] / a1[:, V_DIM:]
            o2 = a2[:, :V_DIM] / a2[:, V_DIM:]
            o_ref[:, hv * V_DIM:(hv + 1) * V_DIM] = _diff_norm(o1, o2, lam, g, lam_init).astype(o_ref.dtype)


def _prompt_attention(q, kt, v, kt_meta, v_meta, lam_vecs, sub_g, *, seq, tq, lam_init):
    rows, attn_dim = q.shape
    n_seq = rows // seq
    nq = seq // tq
    pairs = [(i, j) for i in range(nq) for j in range(i + 1)]
    qi = jnp.asarray([p[0] for p in pairs], jnp.int32)
    kj = jnp.asarray([p[1] for p in pairs], jnp.int32)
    q_blk = pl.BlockSpec((tq, attn_dim), lambda b, s, qi, kj: (b * nq + qi[s], 0))
    v_blk = pl.BlockSpec((tq, attn_dim), lambda b, s, qi, kj: (b * nq + kj[s], 0))
    kt_blk = pl.BlockSpec((None, attn_dim, tq), lambda b, s, qi, kj: (b, 0, kj[s]))
    small = [_full(a.shape) for a in (kt_meta, v_meta, *lam_vecs, sub_g)]
    grid_spec = pltpu.PrefetchScalarGridSpec(
        num_scalar_prefetch=2,
        grid=(n_seq, len(pairs)),
        in_specs=[q_blk, kt_blk, v_blk] + small,
        out_specs=q_blk,
        scratch_shapes=[pltpu.VMEM((N_SUB, tq, LANES), F32), pltpu.VMEM((N_SUB, tq, V_DIM + LANES), F32)],
    )
    return pl.pallas_call(
        functools.partial(_prompt_attn_kernel, tq=tq, tk=tq, lam_init=lam_init),
        grid_spec=grid_spec,
        out_shape=jax.ShapeDtypeStruct((rows, attn_dim), BF16),
        compiler_params=_params("parallel", "arbitrary"),
        name="prompt_attn",
    )(qi, kj, q, kt, v, kt_meta, v_meta, *lam_vecs, sub_g)


def _side_attn_kernel(pt_ref, q_ref, kn_ref, vn_ref, *rest, n, pages_per_step, lam_init):
    del pt_ref
    k_refs = rest[:pages_per_step]
    v_refs = rest[pages_per_step:2 * pages_per_step]
    lq1, lk1, lq2, lk2, sg_ref, o_ref, qbd_sc, m_sc, l_sc, acc_sc = rest[2 * pages_per_step:]
    g = pl.program_id(1)
    rows = N_SUB * n
    attn_dim = q_ref.shape[1]

    @pl.when(g == 0)
    def _():
        qt = jnp.concatenate([q_ref[...]] * N_SUB, axis=0)
        row = lax.broadcasted_iota(jnp.int32, (rows, attn_dim), 0)
        lane = lax.broadcasted_iota(jnp.int32, (rows, attn_dim), 1)
        own = (row // n) == (lane // HEAD_DIM)
        qbd_sc[...] = jnp.where(own, qt, 0.0).astype(qbd_sc.dtype)
        m_sc[...] = jnp.full(m_sc.shape, NEG_INF, F32)
        l_sc[...] = jnp.zeros(l_sc.shape, F32)
        acc_sc[...] = jnp.zeros(acc_sc.shape, F32)

    def update(s, value_block):
        p, alpha, m_new, l_new = _online_softmax(s, m_sc[...], l_sc[...])
        m_sc[...] = m_new
        l_sc[...] = l_new
        for hv in range(N_HEADS):
            pair = slice(2 * hv * n, 2 * (hv + 1) * n)
            acc = acc_sc[pair, :] * alpha[pair]
            for c in range(s.shape[1] // LANES):
                acc = acc + jnp.dot(p[pair, c * LANES:(c + 1) * LANES].astype(BF16),
                                    value_block(hv, c).astype(BF16), preferred_element_type=F32)
            acc_sc[pair, :] = acc

    if pages_per_step:
        qbd = qbd_sc[...]
        s = jnp.concatenate([jnp.dot(qbd, k_ref[...].astype(BF16), preferred_element_type=F32)
                             for k_ref in k_refs], axis=1)
        update(s, lambda hv, c: v_refs[c][pl.ds(hv, LANES, stride=N_HEADS), :])

    @pl.when(g == pl.num_programs(1) - 1)
    def _():
        pad = jnp.zeros((LANES - n, attn_dim), F32)
        kn = jnp.concatenate([kn_ref[...], pad], axis=0).astype(BF16)
        vn = jnp.concatenate([vn_ref[...], pad], axis=0)
        row = lax.broadcasted_iota(jnp.int32, (rows, LANES), 0)
        col = lax.broadcasted_iota(jnp.int32, (rows, LANES), 1)
        s = jnp.where(col <= (row % n), _dot_nt(qbd_sc[...], kn), NEG_INF)
        update(s, lambda hv, c: vn[:, hv * V_DIM:(hv + 1) * V_DIM])
        lam = _lam(lq1, lk1, lq2, lk2, lam_init)
        gain = sg_ref[...]
        out = acc_sc[...] / l_sc[...]
        for hv in range(N_HEADS):
            r0 = 2 * hv * n
            o_ref[:, hv * V_DIM:(hv + 1) * V_DIM] = _diff_norm(out[r0:r0 + n], out[r0 + n:r0 + 2 * n],
                                                               lam, gain, lam_init)


def _side_attention(q, k_new, v_new, lam_vecs, sub_g, *, n, lam_init, cache=None, pages_per_step=0):
    rows, attn_dim = q.shape
    n_seq = rows // n
    tok_blk = pl.BlockSpec((n, attn_dim), lambda b, g, pt: (b, 0))
    small = [_full(a.shape) for a in (*lam_vecs, sub_g)]
    if cache is None:
        page_table = jnp.zeros((1,), jnp.int32)
        steps, kv_specs, kv_args = 1, [], []
    else:
        cache_k, cache_v, layer, page_table = cache
        n_pages = page_table.shape[1]
        steps = n_pages // pages_per_step
        page_table = page_table.reshape(-1)

        def page_blk(jj):
            return pl.BlockSpec(
                (None, None) + cache_k.shape[2:],
                lambda b, g, pt: (layer, pt[b * n_pages + g * pages_per_step + jj], 0, 0))

        kv_specs = [page_blk(jj) for jj in range(pages_per_step)] * 2
        kv_args = [cache_k] * pages_per_step + [cache_v] * pages_per_step
    grid_spec = pltpu.PrefetchScalarGridSpec(
        num_scalar_prefetch=1,
        grid=(n_seq, steps),
        in_specs=[tok_blk, tok_blk, tok_blk] + kv_specs + small,
        out_specs=tok_blk,
        scratch_shapes=[pltpu.VMEM((N_SUB * n, attn_dim), BF16), pltpu.VMEM((N_SUB * n, LANES), F32),
                        pltpu.VMEM((N_SUB * n, LANES), F32), pltpu.VMEM((N_SUB * n, V_DIM), F32)],
    )
    return pl.pallas_call(
        functools.partial(_side_attn_kernel, n=n, pages_per_step=pages_per_step if cache else 0,
                          lam_init=lam_init),
        grid_spec=grid_spec,
        out_shape=jax.ShapeDtypeStruct((rows, attn_dim), F32),
        compiler_params=_params("parallel", "arbitrary"),
        name="sample_attn" if cache else "meta_attn",
    )(page_table, q, k_new, v_new, *kv_args, *lam_vecs, sub_g)


def _post_kernel(yc_ref, ya_ref, x_ref, wo_ref, wg_ref, wu_ref, wd_ref, g1, b1, g2, b2,
                 o_ref, x1_sc, x1b_sc, acc_sc, *, alpha, conv_dim):
    mix = jnp.dot(yc_ref[...].astype(BF16), wo_ref[0:conv_dim, :], preferred_element_type=F32)
    mix = mix + jnp.dot(ya_ref[...].astype(BF16), wo_ref[conv_dim:, :], preferred_element_type=F32)
    x1 = _layernorm(alpha * x_ref[...] + mix, g1[...], b1[...])
    x1_sc[...] = x1
    x1b_sc[...] = x1.astype(BF16)
    acc_sc[...] = jnp.zeros(acc_sc.shape, F32)

    def chunk(c, carry):
        x1b = x1b_sc[...]
        gate = jnp.dot(x1b, wg_ref[c], preferred_element_type=F32)
        up = jnp.dot(x1b, wu_ref[c], preferred_element_type=F32)
        hidden = (gate / (1.0 + jnp.exp(-gate))) * up
        acc_sc[...] += jnp.dot(hidden.astype(BF16), wd_ref[c], preferred_element_type=F32)
        return carry

    lax.fori_loop(0, wg_ref.shape[0], chunk, 0)
    o_ref[...] = _layernorm(alpha * x1_sc[...] + acc_sc[...], g2[...], b2[...])


def _post_attention(yc, ya, x, wo, wg, wu, wd, g1, b1, g2, b2, *, tm, alpha):
    rows, d_model = x.shape
    conv_dim = yc.shape[1]
    row_blk = lambda width: pl.BlockSpec((tm, width), lambda i: (i, 0))
    consts = (wo, wg, wu, wd, g1, b1, g2, b2)
    return pl.pallas_call(
        functools.partial(_post_kernel, alpha=alpha, conv_dim=conv_dim),
        grid=(rows // tm,),
        in_specs=[row_blk(conv_dim), row_blk(ya.shape[1]), row_blk(d_model)] + [_full(a.shape) for a in consts],
        out_specs=row_blk(d_model),
        out_shape=jax.ShapeDtypeStruct((rows, d_model), F32),
        scratch_shapes=[pltpu.VMEM((tm, d_model), F32), pltpu.VMEM((tm, d_model), BF16),
                        pltpu.VMEM((tm, d_model), F32)],
        compiler_params=_params("parallel"),
        name="post_main" if rows > tm else "post_side",
    )(yc, ya, x, *consts)


def _rope_tables(pos):
    inv = 1.0 / (ROPE_THETA ** (jnp.arange(0, HEAD_DIM, 2, dtype=F32) / HEAD_DIM))
    ang = pos.astype(F32)[:, None] * inv[None, :]
    cos = jnp.cos(ang)
    sin = jnp.sin(ang)
    return jnp.concatenate([cos] * 4, axis=-1), jnp.concatenate([-sin, sin] * 2, axis=-1)


def _row_tile(n, target):
    t = min(n, target)
    while n % t:
        t //= 2
    return t


def kernel(x_prompt, x_sample, cache_k, cache_v, state_conv, page_table, meta_tokens, w_in, conv_w, w_out, lambda_q1, lambda_k1, lambda_q2, lambda_k2, subln_g, ln1_g, ln1_b, ln2_g, ln2_b, w_gate_up, w_down):
    batch, seq, d_model = x_prompt.shape
    dec_batch, dec_seq, _ = x_sample.shape
    depth = w_in.shape[0]
    conv_dim = conv_w.shape[2]
    attn_dim = N_SUB * HEAD_DIM
    d_ff = w_down.shape[1]
    n_pool, page_size = cache_k.shape[1], cache_k.shape[2]
    past_len = page_table.shape[1] * page_size
    alpha = (2 * depth) ** 0.25
    n_sample = dec_batch * dec_seq
    n_side = n_sample + N_META
    n_chunks = d_ff // FF_CHUNK
    tm = _row_tile(seq, 512)
    pages_per_step = _row_tile(page_table.shape[1], 16)
    assert page_size == LANES and attn_dim == conv_dim

    w_in_b = w_in.astype(BF16)
    k_col = 3 * conv_dim + attn_dim
    w_main = jnp.concatenate([w_in_b[:, :, :k_col], w_in_b[:, :, k_col + attn_dim:]], axis=2)
    w_kt = w_in_b[:, :, k_col:k_col + attn_dim].transpose(0, 2, 1)
    w_out_b = w_out.astype(BF16)
    w_gate = w_gate_up[:, :, :d_ff].reshape(depth, d_model, n_chunks, FF_CHUNK).transpose(0, 2, 1, 3).astype(BF16)
    w_up = w_gate_up[:, :, d_ff:].reshape(depth, d_model, n_chunks, FF_CHUNK).transpose(0, 2, 1, 3).astype(BF16)
    w_down_b = w_down.reshape(depth, n_chunks, FF_CHUNK, d_model).astype(BF16)
    cache_kt = cache_k.transpose(0, 1, 3, 4, 2).reshape(depth, n_pool, attn_dim, page_size)
    cache_vi = cache_v.reshape(depth, n_pool, page_size * N_HEADS, V_DIM)

    cos_m, sin_m = _rope_tables(N_META + jnp.arange(seq, dtype=jnp.int32))
    cost_m, sint_m = cos_m[:, :HEAD_DIM].T, sin_m[:, :HEAD_DIM].T
    pos_side = jnp.concatenate([jnp.tile(past_len + jnp.arange(dec_seq, dtype=jnp.int32), dec_batch),
                                jnp.arange(N_META, dtype=jnp.int32)])
    cos_s, sin_s = _rope_tables(pos_side)

    j_in_seq = jnp.concatenate([jnp.tile(jnp.arange(dec_seq, dtype=jnp.int32), dec_batch),
                                jnp.arange(N_META, dtype=jnp.int32)])
    sel1 = (j_in_seq < 1).astype(F32)[:, None]
    sel2 = (j_in_seq < 2).astype(F32)[:, None]

    def prefixes(state):
        z = jnp.zeros((dec_batch, dec_seq, conv_dim), F32)
        p1 = z.at[:, 0].set(state[:, 1]).reshape(n_sample, conv_dim)
        p2 = z.at[:, 0].set(state[:, 0]).at[:, 1].set(state[:, 1]).reshape(n_sample, conv_dim)
        zm = jnp.zeros((N_META, conv_dim), F32)
        return jnp.concatenate([p1, zm]), jnp.concatenate([p2, zm])

    x_main = x_prompt.reshape(batch * seq, d_model)
    x_side = jnp.concatenate([x_sample.reshape(n_sample, d_model), meta_tokens.astype(F32)])

    outs = {name: [] for name in ("kp", "vp", "cp", "ks", "vs", "cs")}
    for l in range(depth):
        lam_init = 0.8 - 0.6 * math.exp(-0.3 * l)
        lam_vecs = tuple(a[l][None, :] for a in (lambda_q1, lambda_k1, lambda_q2, lambda_k2))
        sub_g = subln_g[l][None, :]
        row = lambda a: a[l][None, :]

        pre1, pre2 = prefixes(state_conv[l])
        yc_s, q_s, k_s, v_s, u_s = _inproj_side(x_side, w_in_b[l], conv_w[l], cos_s, sin_s, pre1, pre2, sel1, sel2)
        k_meta, v_meta, u_meta = k_s[n_sample:], v_s[n_sample:], u_s[n_sample:]

        yc_m, q_m, kt_m, v_m, ktb_m, vb_m, cstate = _inproj_main(
            x_main, w_main[l], w_kt[l], conv_w[l], cos_m, sin_m, cost_m, sint_m,
            u_meta[N_META - (CONV_W - 1):], seq=seq, tm=tm)

        pad = jnp.zeros((LANES - N_META, attn_dim), BF16)
        ya_m = _prompt_attention(q_m, ktb_m, vb_m,
                                 jnp.concatenate([k_meta.astype(BF16), pad]).T,
                                 jnp.concatenate([v_meta.astype(BF16), pad]),
                                 lam_vecs, sub_g, seq=seq, tq=tm, lam_init=lam_init)
        ya_sample = _side_attention(q_s[:n_sample], k_s[:n_sample], v_s[:n_sample], lam_vecs, sub_g,
                                    n=dec_seq, lam_init=lam_init,
                                    cache=(cache_kt, cache_vi, l, page_table), pages_per_step=pages_per_step)
        ya_meta = _side_attention(q_s[n_sample:], k_meta, v_meta, lam_vecs, sub_g, n=N_META, lam_init=lam_init)
        ya_s = jnp.concatenate([ya_sample, ya_meta])

        post = functools.partial(_post_attention, wo=w_out_b[l], wg=w_gate[l], wu=w_up[l], wd=w_down_b[l],
                                 g1=row(ln1_g), b1=row(ln1_b), g2=row(ln2_g), b2=row(ln2_b), alpha=alpha)
        x_main = post(yc_m, ya_m, x_main, tm=tm)
        x_side = post(yc_s, ya_s, x_side, tm=n_side)

        kt_all = jnp.concatenate([jnp.broadcast_to(k_meta.T[None], (batch, attn_dim, N_META)), kt_m], axis=2)
        outs["kp"].append(kt_all.reshape(batch, N_SUB, HEAD_DIM, N_META + seq).transpose(0, 3, 1, 2))
        v_meta4 = v_meta.reshape(N_META, N_HEADS, V_DIM)
        outs["vp"].append(jnp.concatenate([jnp.broadcast_to(v_meta4[None], (batch,) + v_meta4.shape),
                                           v_m.reshape(batch, seq, N_HEADS, V_DIM)], axis=1))
        outs["cp"].append(cstate)
        outs["ks"].append(k_s[:n_sample].reshape(dec_batch, dec_seq, N_SUB, HEAD_DIM))
        outs["vs"].append(v_s[:n_sample].reshape(dec_batch, dec_seq, N_HEADS, V_DIM))
        outs["cs"].append(u_s[:n_sample].reshape(dec_batch, dec_seq, conv_dim)[:, dec_seq - (CONV_W - 1):])

    y_prompt = x_main.reshape(batch, seq, d_model)
    y_sample = x_side[:n_sample].reshape(dec_batch, dec_seq, d_model)
    return (y_prompt, y_sample, jnp.stack(outs["kp"]), jnp.stack(outs["vp"]), jnp.stack(outs["cp"]),
            jnp.stack(outs["ks"]), jnp.stack(outs["vs"]), jnp.stack(outs["cs"]))
```

```python
import functools
import math

import jax
import jax.numpy as jnp
from jax import lax
from jax.experimental import pallas as pl
from jax.experimental.pallas import tpu as pltpu

F32 = jnp.float32
BF16 = jnp.bfloat16

N_META = 16
CONV_W = 3
N_HEADS = 4
N_SUB = 2 * N_HEADS
HEAD_DIM = 64
V_DIM = 2 * HEAD_DIM
ROPE_THETA = 10000.0
LN_EPS = 1e-5
NEG_INF = -1e30
LOG2_E = math.log2(math.e)
LANES = 128
FF_CHUNK = 256
ATTN_TQ = 1024
ATTN_TK = 512
VMEM_LIMIT = 56 * 1024 * 1024


def _params(*sem):
    return pltpu.CompilerParams(dimension_semantics=sem, vmem_limit_bytes=VMEM_LIMIT)


def _full(shape):
    return pl.BlockSpec(shape, lambda *_: (0,) * len(shape))


def _rope128(x, cos, sin_signed):
    lane = lax.broadcasted_iota(jnp.int32, x.shape, 1)
    swapped = jnp.where((lane & (HEAD_DIM // 2)) == 0,
                        pltpu.roll(x, LANES - HEAD_DIM // 2, 1), pltpu.roll(x, HEAD_DIM // 2, 1))
    return x * cos + swapped * sin_signed


def _layernorm(h, g, b):
    mu = jnp.mean(h, axis=-1, keepdims=True)
    d = h - mu
    var = jnp.mean(d * d, axis=-1, keepdims=True)
    return d * lax.rsqrt(var + LN_EPS) * g + b


def _lam(lq1, lk1, lq2, lk2, lam_init):
    a = jnp.sum(lq1[...] * lk1[...], axis=1, keepdims=True)
    b = jnp.sum(lq2[...] * lk2[...], axis=1, keepdims=True)
    return jnp.exp(a) - jnp.exp(b) + lam_init


def _diff_norm(o1, o2, lam, g, lam_init):
    o = o1 - lam * o2
    o = o * lax.rsqrt(jnp.mean(o * o, axis=-1, keepdims=True) + LN_EPS)
    return o * g * (1.0 - lam_init)


def _lanes(x, width):
    reps = width // LANES
    return x if reps == 1 else jnp.concatenate([x] * reps, axis=1)


def _online_softmax(s, m_prev, l_prev):
    m_new = jnp.maximum(m_prev, jnp.max(s, axis=1, keepdims=True))
    alpha = jnp.exp(m_prev - m_new)
    p = jnp.exp(s - _lanes(m_new, s.shape[1]))
    l_new = alpha * l_prev + jnp.sum(p, axis=1, keepdims=True)
    return p, alpha, m_new, l_new


def _dot_nt(a, b):
    return lax.dot_general(a, b, (((1,), (1,)), ((), ())), preferred_element_type=F32)


def _project(x_ref, w_ref, conv_dim):
    xb = x_ref[...].astype(BF16)

    def proj(seg):
        return jnp.dot(xb, w_ref[:, seg * conv_dim:(seg + 1) * conv_dim], preferred_element_type=F32)

    return proj


def _rope_store(val, cos, sin, scale, refs):
    for c in range(val.shape[1] // LANES):
        sl = slice(c * LANES, (c + 1) * LANES)
        r = _rope128(val[:, sl], cos, sin)
        if scale != 1.0:
            r = r * scale
        for ref in refs:
            ref[:, sl] = r.astype(ref.dtype)


def _inproj_main_kernel(x_ref, w_ref, wkt_ref, cw_ref, cos_ref, sin_ref, cost_ref, sint_ref, upre_ref,
                        *rest, tm, tiles_per_seq, conv_dim):
    yc_ref, q_ref, kt_ref, v_ref, ktb_ref, vb_ref, cs_ref, ubuf = rest[-8:]
    i = pl.program_id(0)
    proj = _project(x_ref, w_ref, conv_dim)
    u = proj(2) * proj(0)

    @pl.when(i % tiles_per_seq == 0)
    def _():
        ubuf[6:8, :] = upre_ref[...]

    ubuf[8:8 + tm, :] = u
    cw = cw_ref[...]
    conv = ubuf[6:6 + tm, :] * cw[0:1] + ubuf[7:7 + tm, :] * cw[1:2]
    conv = conv + ubuf[8:8 + tm, :] * cw[2:3]
    yc_ref[...] = (proj(1) * conv).astype(yc_ref.dtype)
    tail = ubuf[tm + 6:tm + 8, :]
    cs_ref[...] = tail
    ubuf[6:8, :] = tail

    cos = cos_ref[...]
    sin = sin_ref[...]
    _rope_store(proj(3), cos, sin, HEAD_DIM ** -0.5 * LOG2_E, (q_ref,))

    kt = _dot_nt(wkt_ref[...], x_ref[...].astype(BF16))
    cost = cost_ref[...]
    sint = sint_ref[...]
    half = HEAD_DIM // 2
    for h in range(N_SUB):
        blk = kt[h * HEAD_DIM:(h + 1) * HEAD_DIM, :]
        swapped = jnp.concatenate([blk[half:], blk[:half]], axis=0)
        r = blk * cost + swapped * sint
        kt_ref[h * HEAD_DIM:(h + 1) * HEAD_DIM, :] = r
        ktb_ref[h * HEAD_DIM:(h + 1) * HEAD_DIM, :] = r.astype(ktb_ref.dtype)

    vv = proj(4)
    vb_ref[...] = vv.astype(vb_ref.dtype)
    for hv in range(N_HEADS):
        v_ref[0, 0, pl.ds(hv, tm, stride=N_HEADS), :] = vv[:, hv * V_DIM:(hv + 1) * V_DIM]


def _inproj_side_kernel(x_ref, w_ref, cw_ref, cos_ref, sin_ref, pre1_ref, pre2_ref, sel1_ref, sel2_ref,
                        yc_ref, q_ref, k_ref, v_ref, u_ref, ubuf, *, tm, conv_dim):
    proj = _project(x_ref, w_ref, conv_dim)
    u = proj(2) * proj(0)
    u_ref[...] = u
    ubuf[0:8, :] = jnp.zeros((8, conv_dim), F32)
    ubuf[8:8 + tm, :] = u
    um2 = jnp.where(sel2_ref[...] > 0.5, pre2_ref[...], ubuf[6:6 + tm, :])
    um1 = jnp.where(sel1_ref[...] > 0.5, pre1_ref[...], ubuf[7:7 + tm, :])
    cw = cw_ref[...]
    conv = um2 * cw[0:1] + um1 * cw[1:2]
    conv = conv + u * cw[2:3]
    yc_ref[...] = proj(1) * conv
    cos = cos_ref[...]
    sin = sin_ref[...]
    _rope_store(proj(3), cos, sin, HEAD_DIM ** -0.5, (q_ref,))
    _rope_store(proj(4), cos, sin, 1.0, (k_ref,))
    v_ref[...] = proj(5)


def _inproj_main(x, w, wkt, cw, cos, sin, cost, sint, upre, cache_out, *, seq, tm, layer, depth):
    rows, d_model = x.shape
    conv_dim = cw.shape[1]
    attn_dim = wkt.shape[0]
    tiles_per_seq = seq // tm
    n_seq = rows // seq
    row_blk = lambda width: pl.BlockSpec((tm, width), lambda i: (i, 0))
    tab_blk = pl.BlockSpec((tm, LANES), lambda i: (i % tiles_per_seq, 0))
    tabt_blk = pl.BlockSpec((HEAD_DIM, tm), lambda i: (0, i % tiles_per_seq))
    kt_blk = pl.BlockSpec((None, attn_dim, tm), lambda i: (i // tiles_per_seq, 0, i % tiles_per_seq))
    kt_out_blk = pl.BlockSpec((None, None, attn_dim, tm),
                              lambda i: (layer, i // tiles_per_seq, 0, i % tiles_per_seq))
    v_out_blk = pl.BlockSpec((pl.Element(1), pl.Element(1), pl.Element(tm * N_HEADS), pl.Element(V_DIM)),
                             lambda i: (layer, i // tiles_per_seq,
                                        pl.multiple_of((N_META + (i % tiles_per_seq) * tm) * N_HEADS, 8), 0))
    out_shape = (
        jax.ShapeDtypeStruct((rows, conv_dim), BF16),
        jax.ShapeDtypeStruct((rows, attn_dim), BF16),
        jax.ShapeDtypeStruct((depth, n_seq, attn_dim, seq), F32),
        jax.ShapeDtypeStruct((depth, n_seq, (N_META + seq) * N_HEADS, V_DIM), F32),
        jax.ShapeDtypeStruct((n_seq, attn_dim, seq), BF16),
        jax.ShapeDtypeStruct((rows, attn_dim), BF16),
        jax.ShapeDtypeStruct((n_seq, CONV_W - 1, conv_dim), F32),
    )
    args = [x, w, wkt, cw, cos, sin, cost, sint, upre]
    in_specs = [row_blk(d_model), _full(w.shape), _full(wkt.shape), _full(cw.shape), tab_blk, tab_blk,
                tabt_blk, tabt_blk, _full(upre.shape)]
    aliases = {}
    if cache_out is not None:
        aliases = {len(args): 2, len(args) + 1: 3}
        args += list(cache_out)
        in_specs += [pl.BlockSpec(memory_space=pl.ANY)] * 2
    return pl.pallas_call(
        functools.partial(_inproj_main_kernel, tm=tm, tiles_per_seq=tiles_per_seq, conv_dim=conv_dim),
        grid=(rows // tm,),
        in_specs=in_specs,
        out_specs=(row_blk(conv_dim), row_blk(attn_dim), kt_out_blk, v_out_blk, kt_blk, row_blk(attn_dim),
                   pl.BlockSpec((None, CONV_W - 1, conv_dim), lambda i: (i // tiles_per_seq, 0, 0))),
        out_shape=out_shape,
        input_output_aliases=aliases,
        scratch_shapes=[pltpu.VMEM((tm + 8, conv_dim), F32)],
        compiler_params=_params("arbitrary"),
        name="inproj_main",
    )(*args)


def _inproj_side(x, w, cw, cos, sin, pre1, pre2, sel1, sel2):
    rows, _ = x.shape
    conv_dim = cw.shape[1]
    args = (x, w, cw, cos, sin, pre1, pre2, sel1, sel2)
    out = jax.ShapeDtypeStruct((rows, conv_dim), F32)
    return pl.pallas_call(
        functools.partial(_inproj_side_kernel, tm=rows, conv_dim=conv_dim),
        grid=(1,),
        in_specs=[_full(a.shape) for a in args],
        out_specs=tuple(_full(out.shape) for _ in range(5)),
        out_shape=(out,) * 5,
        scratch_shapes=[pltpu.VMEM((rows + 8, conv_dim), F32)],
        compiler_params=_params("arbitrary"),
        name="inproj_side",
    )(*args)


def _prompt_attn_kernel(qi_ref, kj_ref, q_ref, kt_ref, v_ref, kmt_ref, vm_ref,
                        lq1, lk1, lq2, lk2, sg_ref, o_ref, m_sc, acc_sc, *, tq, tk, lam_init):
    step = pl.program_id(1)
    i = qi_ref[step]
    j = kj_ref[step]

    def head_slices(h):
        return slice(h * HEAD_DIM, (h + 1) * HEAD_DIM), slice((h // 2) * V_DIM, (h // 2 + 1) * V_DIM)

    def pv(p, v_blk):
        ones = jnp.ones((v_blk.shape[0], LANES), BF16)
        return jnp.dot(p.astype(BF16), jnp.concatenate([v_blk, ones], axis=1), preferred_element_type=F32)

    @pl.when(j == 0)
    def _():
        col = lax.broadcasted_iota(jnp.int32, (tq, LANES), 1)
        for h in range(N_SUB):
            qk, vv = head_slices(h)
            s = jnp.dot(q_ref[:, qk], kmt_ref[qk, :], preferred_element_type=F32)
            s = jnp.where(col < N_META, s, NEG_INF)
            m = jnp.max(s, axis=1, keepdims=True)
            m_sc[h] = jnp.broadcast_to(m, (tq, LANES))
            acc_sc[h] = pv(jnp.exp2(s - m), vm_ref[:, vv])

    def process(rows, triangular):
        n_rows = rows.stop - rows.start
        if triangular:
            row = lax.broadcasted_iota(jnp.int32, (n_rows, tk), 0)
            col = lax.broadcasted_iota(jnp.int32, (n_rows, tk), 1)
            visible = col <= row
        for h in range(N_SUB):
            qk, vv = head_slices(h)
            s = jnp.dot(q_ref[rows, qk], kt_ref[qk, :], preferred_element_type=F32)
            if triangular:
                s = jnp.where(visible, s, NEG_INF)
            m_prev = m_sc[h, rows]
            m_new = jnp.maximum(m_prev, jnp.max(s, axis=1, keepdims=True))
            alpha = jnp.exp2(m_prev - m_new)
            p = jnp.exp2(s - _lanes(m_new, tk))
            m_sc[h, rows] = m_new
            acc_sc[h, rows] = acc_sc[h, rows] * _lanes(alpha, V_DIM + LANES) + pv(p, v_ref[:, vv])

    per_q = tq // tk
    j_local = j - i * per_q

    @pl.when(j_local < 0)
    def _():
        process(slice(0, tq), False)

    for jl in range(per_q):
        @pl.when(j_local == jl)
        def _(jl=jl):
            process(slice(jl * tk, (jl + 1) * tk), True)
            if jl + 1 < per_q:
                process(slice((jl + 1) * tk, tq), False)

    @pl.when(j_local == per_q - 1)
    def _():
        lam = _lam(lq1, lk1, lq2, lk2, lam_init)
        g = sg_ref[...]
        for hv in range(N_HEADS):
            a1 = acc_sc[2 * hv]
            a2 = acc_sc[2 * hv + 1]
            o1 = a1[:, :V_DIM] / a1[:, V_DIM:]
            o2 = a2[:, :V_DIM] / a2[:, V_DIM:]
            o_ref[:, hv * V_DIM:(hv + 1) * V_DIM] = _diff_norm(o1, o2, lam, g, lam_init).astype(o_ref.dtype)


def _prompt_attention(q, kt, v, kt_meta, v_meta, lam_vecs, sub_g, *, seq, tq, tk, lam_init):
    rows, attn_dim = q.shape
    n_seq = rows // seq
    nq, nk = seq // tq, seq // tk
    per_q = tq // tk
    pairs = [(i, j) for i in range(nq) for j in range((i + 1) * per_q)]
    qi = jnp.asarray([p[0] for p in pairs], jnp.int32)
    kj = jnp.asarray([p[1] for p in pairs], jnp.int32)
    q_blk = pl.BlockSpec((tq, attn_dim), lambda b, s, qi, kj: (b * nq + qi[s], 0))
    v_blk = pl.BlockSpec((tk, attn_dim), lambda b, s, qi, kj: (b * nk + kj[s], 0))
    kt_blk = pl.BlockSpec((None, attn_dim, tk), lambda b, s, qi, kj: (b, 0, kj[s]))
    small = [_full(a.shape) for a in (kt_meta, v_meta, *lam_vecs, sub_g)]
    grid_spec = pltpu.PrefetchScalarGridSpec(
        num_scalar_prefetch=2,
        grid=(n_seq, len(pairs)),
        in_specs=[q_blk, kt_blk, v_blk] + small,
        out_specs=q_blk,
        scratch_shapes=[pltpu.VMEM((N_SUB, tq, LANES), F32), pltpu.VMEM((N_SUB, tq, V_DIM + LANES), F32)],
    )
    return pl.pallas_call(
        functools.partial(_prompt_attn_kernel, tq=tq, tk=tk, lam_init=lam_init),
        grid_spec=grid_spec,
        out_shape=jax.ShapeDtypeStruct((rows, attn_dim), BF16),
        compiler_params=_params("parallel", "arbitrary"),
        name="prompt_attn",
    )(qi, kj, q, kt, v, kt_meta, v_meta, *lam_vecs, sub_g)


def _side_attn_kernel(pt_ref, q_ref, kn_ref, vn_ref, *rest, n, pages_per_step, lam_init):
    del pt_ref
    k_refs = rest[:pages_per_step]
    v_refs = rest[pages_per_step:2 * pages_per_step]
    lq1, lk1, lq2, lk2, sg_ref, o_ref, qbd_sc, m_sc, l_sc, acc_sc = rest[2 * pages_per_step:]
    g = pl.program_id(1)
    rows = N_SUB * n
    attn_dim = q_ref.shape[1]

    @pl.when(g == 0)
    def _():
        qt = jnp.concatenate([q_ref[...]] * N_SUB, axis=0)
        row = lax.broadcasted_iota(jnp.int32, (rows, attn_dim), 0)
        lane = lax.broadcasted_iota(jnp.int32, (rows, attn_dim), 1)
        own = (row // n) == (lane // HEAD_DIM)
        qbd_sc[...] = jnp.where(own, qt, 0.0).astype(qbd_sc.dtype)
        m_sc[...] = jnp.full(m_sc.shape, NEG_INF, F32)
        l_sc[...] = jnp.zeros(l_sc.shape, F32)
        acc_sc[...] = jnp.zeros(acc_sc.shape, F32)

    def update(s, value_block):
        p, alpha, m_new, l_new = _online_softmax(s, m_sc[...], l_sc[...])
        m_sc[...] = m_new
        l_sc[...] = l_new
        for hv in range(N_HEADS):
            pair = slice(2 * hv * n, 2 * (hv + 1) * n)
            acc = acc_sc[pair, :] * alpha[pair]
            for c in range(s.shape[1] // LANES):
                acc = acc + jnp.dot(p[pair, c * LANES:(c + 1) * LANES].astype(BF16),
                                    value_block(hv, c).astype(BF16), preferred_element_type=F32)
            acc_sc[pair, :] = acc

    if pages_per_step:
        qbd = qbd_sc[...]
        s = jnp.concatenate([jnp.dot(qbd, k_ref[...].astype(BF16), preferred_element_type=F32)
                             for k_ref in k_refs], axis=1)
        update(s, lambda hv, c: v_refs[c][pl.ds(hv, LANES, stride=N_HEADS), :])

    @pl.when(g == pl.num_programs(1) - 1)
    def _():
        pad = jnp.zeros((LANES - n, attn_dim), F32)
        kn = jnp.concatenate([kn_ref[...], pad], axis=0).astype(BF16)
        vn = jnp.concatenate([vn_ref[...], pad], axis=0)
        row = lax.broadcasted_iota(jnp.int32, (rows, LANES), 0)
        col = lax.broadcasted_iota(jnp.int32, (rows, LANES), 1)
        s = jnp.where(col <= (row % n), _dot_nt(qbd_sc[...], kn), NEG_INF)
        update(s, lambda hv, c: vn[:, hv * V_DIM:(hv + 1) * V_DIM])
        lam = _lam(lq1, lk1, lq2, lk2, lam_init)
        gain = sg_ref[...]
        out = acc_sc[...] / l_sc[...]
        for hv in range(N_HEADS):
            r0 = 2 * hv * n
            o_ref[:, hv * V_DIM:(hv + 1) * V_DIM] = _diff_norm(out[r0:r0 + n], out[r0 + n:r0 + 2 * n],
                                                               lam, gain, lam_init)


def _side_attention(q, k_new, v_new, lam_vecs, sub_g, *, n, lam_init, cache=None, pages_per_step=0):
    rows, attn_dim = q.shape
    n_seq = rows // n
    tok_blk = pl.BlockSpec((n, attn_dim), lambda b, g, pt: (b, 0))
    small = [_full(a.shape) for a in (*lam_vecs, sub_g)]
    if cache is None:
        page_table = jnp.zeros((1,), jnp.int32)
        steps, kv_specs, kv_args = 1, [], []
    else:
        cache_k, cache_v, layer, page_table = cache
        n_pages = page_table.shape[1]
        steps = n_pages // pages_per_step
        page_table = page_table.reshape(-1)

        def page_blk(jj):
            return pl.BlockSpec(
                (None, None) + cache_k.shape[2:],
                lambda b, g, pt: (layer, pt[b * n_pages + g * pages_per_step + jj], 0, 0))

        kv_specs = [page_blk(jj) for jj in range(pages_per_step)] * 2
        kv_args = [cache_k] * pages_per_step + [cache_v] * pages_per_step
    grid_spec = pltpu.PrefetchScalarGridSpec(
        num_scalar_prefetch=1,
        grid=(n_seq, steps),
        in_specs=[tok_blk, tok_blk, tok_blk] + kv_specs + small,
        out_specs=tok_blk,
        scratch_shapes=[pltpu.VMEM((N_SUB * n, attn_dim), BF16), pltpu.VMEM((N_SUB * n, LANES), F32),
                        pltpu.VMEM((N_SUB * n, LANES), F32), pltpu.VMEM((N_SUB * n, V_DIM), F32)],
    )
    return pl.pallas_call(
        functools.partial(_side_attn_kernel, n=n, pages_per_step=pages_per_step if cache else 0,
                          lam_init=lam_init),
        grid_spec=grid_spec,
        out_shape=jax.ShapeDtypeStruct((rows, attn_dim), F32),
        compiler_params=_params("parallel", "arbitrary"),
        name="sample_attn" if cache else "meta_attn",
    )(page_table, q, k_new, v_new, *kv_args, *lam_vecs, sub_g)


def _post_kernel(yc_ref, ya_ref, x_ref, wo_ref, wgu_ref, wd_ref, g1, b1, g2, b2,
                 o_ref, x1_sc, x1b_sc, acc_sc, *, alpha, conv_dim):
    d_ff = wd_ref.shape[0]
    mix = jnp.dot(yc_ref[...].astype(BF16), wo_ref[0:conv_dim, :], preferred_element_type=F32)
    mix = mix + jnp.dot(ya_ref[...].astype(BF16), wo_ref[conv_dim:, :], preferred_element_type=F32)
    x1 = _layernorm(alpha * x_ref[...] + mix, g1[...], b1[...])
    x1_sc[...] = x1
    x1b_sc[...] = x1.astype(BF16)
    acc_sc[...] = jnp.zeros(acc_sc.shape, F32)

    for c0 in range(0, d_ff, FF_CHUNK):
        x1b = x1b_sc[...]
        gate = jnp.dot(x1b, wgu_ref[:, c0:c0 + FF_CHUNK], preferred_element_type=F32)
        up = jnp.dot(x1b, wgu_ref[:, d_ff + c0:d_ff + c0 + FF_CHUNK], preferred_element_type=F32)
        hidden = (gate / (1.0 + jnp.exp(-gate))) * up
        acc_sc[...] += jnp.dot(hidden.astype(BF16), wd_ref[c0:c0 + FF_CHUNK, :], preferred_element_type=F32)
    o_ref[...] = _layernorm(alpha * x1_sc[...] + acc_sc[...], g2[...], b2[...])


def _post_attention(yc, ya, x, wo, wgu, wd, g1, b1, g2, b2, *, tm, alpha):
    rows, d_model = x.shape
    conv_dim = yc.shape[1]
    assert wd.shape[0] % FF_CHUNK == 0
    row_blk = lambda width: pl.BlockSpec((tm, width), lambda i: (i, 0))
    consts = (wo, wgu, wd, g1, b1, g2, b2)
    return pl.pallas_call(
        functools.partial(_post_kernel, alpha=alpha, conv_dim=conv_dim),
        grid=(rows // tm,),
        in_specs=[row_blk(conv_dim), row_blk(ya.shape[1]), row_blk(d_model)] + [_full(a.shape) for a in consts],
        out_specs=row_blk(d_model),
        out_shape=jax.ShapeDtypeStruct((rows, d_model), F32),
        scratch_shapes=[pltpu.VMEM((tm, d_model), F32), pltpu.VMEM((tm, d_model), BF16),
                        pltpu.VMEM((tm, d_model), F32)],
        compiler_params=_params("parallel"),
        name="post_main" if rows > tm else "post_side",
    )(yc, ya, x, *consts)


def _rope_tables(pos):
    inv = 1.0 / (ROPE_THETA ** (jnp.arange(0, HEAD_DIM, 2, dtype=F32) / HEAD_DIM))
    ang = pos.astype(F32)[:, None] * inv[None, :]
    cos = jnp.cos(ang)
    sin = jnp.sin(ang)
    return jnp.concatenate([cos] * 4, axis=-1), jnp.concatenate([-sin, sin] * 2, axis=-1)


def _row_tile(n, target):
    t = min(n, target)
    while n % t:
        t //= 2
    return t


def kernel(x_prompt, x_sample, cache_k, cache_v, state_conv, page_table, meta_tokens, w_in, conv_w, w_out, lambda_q1, lambda_k1, lambda_q2, lambda_k2, subln_g, ln1_g, ln1_b, ln2_g, ln2_b, w_gate_up, w_down):
    batch, seq, d_model = x_prompt.shape
    dec_batch, dec_seq, _ = x_sample.shape
    depth = w_in.shape[0]
    conv_dim = conv_w.shape[2]
    attn_dim = N_SUB * HEAD_DIM
    d_ff = w_down.shape[1]
    n_pool, page_size = cache_k.shape[1], cache_k.shape[2]
    past_len = page_table.shape[1] * page_size
    alpha = (2 * depth) ** 0.25
    n_sample = dec_batch * dec_seq
    n_side = n_sample + N_META
    tm = _row_tile(seq, 512)
    tq = _row_tile(seq, ATTN_TQ)
    tk = _row_tile(tq, ATTN_TK)
    pages_per_step = _row_tile(page_table.shape[1], 16)
    assert page_size == LANES and attn_dim == conv_dim

    w_in_b = w_in.astype(BF16)
    k_col = 3 * conv_dim + attn_dim
    w_main = jnp.concatenate([w_in_b[:, :, :k_col], w_in_b[:, :, k_col + attn_dim:]], axis=2)
    w_kt = w_in_b[:, :, k_col:k_col + attn_dim].transpose(0, 2, 1)
    w_out_b = w_out.astype(BF16)
    w_gate_up_b = w_gate_up.astype(BF16)
    w_down_b = w_down.astype(BF16)
    cache_kt = cache_k.transpose(0, 1, 3, 4, 2).reshape(depth, n_pool, attn_dim, page_size)
    cache_vi = cache_v.reshape(depth, n_pool, page_size * N_HEADS, V_DIM)

    cos_m, sin_m = _rope_tables(N_META + jnp.arange(seq, dtype=jnp.int32))
    cost_m, sint_m = cos_m[:, :HEAD_DIM].T, sin_m[:, :HEAD_DIM].T
    pos_side = jnp.concatenate([jnp.tile(past_len + jnp.arange(dec_seq, dtype=jnp.int32), dec_batch),
                                jnp.arange(N_META, dtype=jnp.int32)])
    cos_s, sin_s = _rope_tables(pos_side)

    j_in_seq = jnp.concatenate([jnp.tile(jnp.arange(dec_seq, dtype=jnp.int32), dec_batch),
                                jnp.arange(N_META, dtype=jnp.int32)])
    sel1 = (j_in_seq < 1).astype(F32)[:, None]
    sel2 = (j_in_seq < 2).astype(F32)[:, None]

    def prefixes(state):
        z = jnp.zeros((dec_batch, dec_seq, conv_dim), F32)
        p1 = z.at[:, 0].set(state[:, 1]).reshape(n_sample, conv_dim)
        p2 = z.at[:, 0].set(state[:, 0]).at[:, 1].set(state[:, 1]).reshape(n_sample, conv_dim)
        zm = jnp.zeros((N_META, conv_dim), F32)
        return jnp.concatenate([p1, zm]), jnp.concatenate([p2, zm])

    x_main = x_prompt.reshape(batch * seq, d_model)
    x_side = jnp.concatenate([x_sample.reshape(n_sample, d_model), meta_tokens.astype(F32)])

    outs = {name: [] for name in ("cp", "ks", "vs", "cs")}
    prompt_cache, meta_kt, meta_v = None, [], []
    for l in range(depth):
        lam_init = 0.8 - 0.6 * math.exp(-0.3 * l)
        lam_vecs = tuple(a[l][None, :] for a in (lambda_q1, lambda_k1, lambda_q2, lambda_k2))
        sub_g = subln_g[l][None, :]
        row = lambda a: a[l][None, :]

        pre1, pre2 = prefixes(state_conv[l])
        yc_s, q_s, k_s, v_s, u_s = _inproj_side(x_side, w_in_b[l], conv_w[l], cos_s, sin_s, pre1, pre2, sel1, sel2)
        k_meta, v_meta, u_meta = k_s[n_sample:], v_s[n_sample:], u_s[n_sample:]

        yc_m, q_m, kt_cache, v_cache, ktb_m, vb_m, cstate = _inproj_main(
            x_main, w_main[l], w_kt[l], conv_w[l], cos_m, sin_m, cost_m, sint_m,
            u_meta[N_META - (CONV_W - 1):], prompt_cache, seq=seq, tm=tm, layer=l, depth=depth)
        prompt_cache = (kt_cache, v_cache)
        meta_kt.append(k_meta.T)
        meta_v.append(v_meta.reshape(N_META * N_HEADS, V_DIM))

        pad = jnp.zeros((LANES - N_META, attn_dim), BF16)
        ya_m = _prompt_attention(q_m, ktb_m, vb_m,
                                 jnp.concatenate([k_meta.astype(BF16), pad]).T,
                                 jnp.concatenate([v_meta.astype(BF16), pad]),
                                 lam_vecs, sub_g, seq=seq, tq=tq, tk=tk, lam_init=lam_init)
        ya_sample = _side_attention(q_s[:n_sample], k_s[:n_sample], v_s[:n_sample], lam_vecs, sub_g,
                                    n=dec_seq, lam_init=lam_init,
                                    cache=(cache_kt, cache_vi, l, page_table), pages_per_step=pages_per_step)
        ya_meta = _side_attention(q_s[n_sample:], k_meta, v_meta, lam_vecs, sub_g, n=N_META, lam_init=lam_init)
        ya_s = jnp.concatenate([ya_sample, ya_meta])

        post = functools.partial(_post_attention, wo=w_out_b[l], wgu=w_gate_up_b[l], wd=w_down_b[l],
                                 g1=row(ln1_g), b1=row(ln1_b), g2=row(ln2_g), b2=row(ln2_b), alpha=alpha)
        x_main = post(yc_m, ya_m, x_main, tm=tm)
        x_side = post(yc_s, ya_s, x_side, tm=n_side)

        outs["cp"].append(cstate)
        outs["ks"].append(k_s[:n_sample].reshape(dec_batch, dec_seq, N_SUB, HEAD_DIM))
        outs["vs"].append(v_s[:n_sample].reshape(dec_batch, dec_seq, N_HEADS, V_DIM))
        outs["cs"].append(u_s[:n_sample].reshape(dec_batch, dec_seq, conv_dim)[:, dec_seq - (CONV_W - 1):])

    kt_cache, v_cache = prompt_cache
    meta_kt_b = jnp.broadcast_to(jnp.stack(meta_kt)[:, None], (depth, batch, attn_dim, N_META))
    new_k_prompt = jnp.concatenate([meta_kt_b, kt_cache], axis=3).reshape(
        depth, batch, N_SUB, HEAD_DIM, N_META + seq).transpose(0, 1, 4, 2, 3)
    meta_v_b = jnp.broadcast_to(jnp.stack(meta_v)[:, None], (depth, batch, N_META * N_HEADS, V_DIM))
    new_v_prompt = lax.dynamic_update_slice(v_cache, meta_v_b, (0, 0, 0, 0)).reshape(
        depth, batch, N_META + seq, N_HEADS, V_DIM)

    y_prompt = x_main.reshape(batch, seq, d_model)
    y_sample = x_side[:n_sample].reshape(dec_batch, dec_seq, d_model)
    return (y_prompt, y_sample, new_k_prompt, new_v_prompt, jnp.stack(outs["cp"]),
            jnp.stack(outs["ks"]), jnp.stack(outs["vs"]), jnp.stack(outs["cs"]))
```

```python
import functools
import math

import jax
import jax.numpy as jnp
from jax import lax
from jax.experimental import pallas as pl
from jax.experimental.pallas import tpu as pltpu

F32 = jnp.float32
BF16 = jnp.bfloat16

N_META = 16
CONV_W = 3
N_HEADS = 4
N_SUB = 2 * N_HEADS
HEAD_DIM = 64
V_DIM = 2 * HEAD_DIM
ROPE_THETA = 10000.0
LN_EPS = 1e-5
NEG_INF = -1e30
LOG2_E = math.log2(math.e)
LANES = 128
FF_CHUNK = 256
ATTN_TQ = 2048
ATTN_TK = 512
VMEM_LIMIT = 56 * 1024 * 1024


def _params(*sem):
    return pltpu.CompilerParams(dimension_semantics=sem, vmem_limit_bytes=VMEM_LIMIT)


def _full(shape):
    return pl.BlockSpec(shape, lambda *_: (0,) * len(shape))


def _of_layer(stacked, layer):
    tail = stacked.shape[1:]
    return pl.BlockSpec((None,) + tail, lambda *_: (layer,) + (0,) * len(tail))


def _rope128(x, cos, sin_signed):
    lane = lax.broadcasted_iota(jnp.int32, x.shape, 1)
    swapped = jnp.where((lane & (HEAD_DIM // 2)) == 0,
                        pltpu.roll(x, LANES - HEAD_DIM // 2, 1), pltpu.roll(x, HEAD_DIM // 2, 1))
    return x * cos + swapped * sin_signed


def _layernorm(h, g, b):
    mu = jnp.mean(h, axis=-1, keepdims=True)
    d = h - mu
    var = jnp.mean(d * d, axis=-1, keepdims=True)
    return d * lax.rsqrt(var + LN_EPS) * g + b


def _lam(lq1, lk1, lq2, lk2, lam_init):
    a = jnp.sum(lq1[...] * lk1[...], axis=1, keepdims=True)
    b = jnp.sum(lq2[...] * lk2[...], axis=1, keepdims=True)
    return jnp.exp(a) - jnp.exp(b) + lam_init


def _diff_norm(o1, o2, lam, g, lam_init):
    o = o1 - lam * o2
    o = o * lax.rsqrt(jnp.mean(o * o, axis=-1, keepdims=True) + LN_EPS)
    return o * g * (1.0 - lam_init)


def _lanes(x, width):
    reps = width // LANES
    return x if reps == 1 else jnp.concatenate([x] * reps, axis=1)


def _online_softmax(s, m_prev, l_prev):
    m_new = jnp.maximum(m_prev, jnp.max(s, axis=1, keepdims=True))
    alpha = jnp.exp(m_prev - m_new)
    p = jnp.exp(s - _lanes(m_new, s.shape[1]))
    l_new = alpha * l_prev + jnp.sum(p, axis=1, keepdims=True)
    return p, alpha, m_new, l_new


def _dot_nt(a, b):
    return lax.dot_general(a, b, (((1,), (1,)), ((), ())), preferred_element_type=F32)


def _project(x_ref, w_ref, conv_dim):
    xb = x_ref[...].astype(BF16)

    def proj(seg):
        return jnp.dot(xb, w_ref[:, seg * conv_dim:(seg + 1) * conv_dim], preferred_element_type=F32)

    return proj


def _rope_store(val, cos, sin, scale, refs):
    for c in range(val.shape[1] // LANES):
        sl = slice(c * LANES, (c + 1) * LANES)
        r = _rope128(val[:, sl], cos, sin)
        if scale != 1.0:
            r = r * scale
        for ref in refs:
            ref[:, sl] = r.astype(ref.dtype)


def _inproj_main_kernel(x_ref, w_ref, wkt_ref, cw_ref, cos_ref, sin_ref, cost_ref, sint_ref, upre_ref,
                        kmt_ref, *rest, tm, tiles_per_seq, conv_dim):
    yc_ref, q_ref, kt_ref, v_ref, ktb_ref, vb_ref, cs_ref, ktail_ref, ubuf = rest[-9:]

    @pl.when(pl.program_id(0) % tiles_per_seq == 0)
    def _():
        ubuf[6:8, :] = upre_ref[...]
        ktail_ref[...] = kmt_ref[...]

    proj = _project(x_ref, w_ref, conv_dim)
    u = proj(2) * proj(0)
    ubuf[8:8 + tm, :] = u
    cw = cw_ref[...]
    conv = ubuf[6:6 + tm, :] * cw[0:1] + ubuf[7:7 + tm, :] * cw[1:2]
    conv = conv + ubuf[8:8 + tm, :] * cw[2:3]
    yc_ref[...] = (proj(1) * conv).astype(yc_ref.dtype)
    tail = ubuf[tm + 6:tm + 8, :]
    cs_ref[...] = tail
    ubuf[6:8, :] = tail

    cos = cos_ref[...]
    sin = sin_ref[...]
    _rope_store(proj(3), cos, sin, HEAD_DIM ** -0.5 * LOG2_E, (q_ref,))

    kt = _dot_nt(wkt_ref[...], x_ref[...].astype(BF16))
    cost = cost_ref[...]
    sint = sint_ref[...]
    half = HEAD_DIM // 2
    lane = lax.broadcasted_iota(jnp.int32, (HEAD_DIM, LANES), 1)
    for h in range(N_SUB):
        hd = slice(h * HEAD_DIM, (h + 1) * HEAD_DIM)
        blk = kt[hd, :]
        swapped = jnp.concatenate([blk[half:], blk[:half]], axis=0)
        r = blk * cost + swapped * sint
        ktb_ref[hd, :] = r.astype(ktb_ref.dtype)
        shifted = pltpu.roll(r, N_META, 1)
        kt_ref[hd, :LANES] = jnp.where(lane < N_META, ktail_ref[hd, :], shifted[:, :LANES])
        kt_ref[hd, LANES:] = shifted[:, LANES:]
        ktail_ref[hd, :] = shifted[:, :LANES]

    vv = proj(5)
    vb_ref[...] = vv.astype(vb_ref.dtype)
    for hv in range(N_HEADS):
        v_ref[0, 0, pl.ds(hv, tm, stride=N_HEADS), :] = vv[:, hv * V_DIM:(hv + 1) * V_DIM]


def _inproj_side_kernel(x_ref, w_ref, cw_ref, cos_ref, sin_ref, pre1_ref, pre2_ref, sel1_ref, sel2_ref,
                        yc_ref, q_ref, k_ref, v_ref, u_ref, ubuf, *, tm, conv_dim):
    proj = _project(x_ref, w_ref, conv_dim)
    u = proj(2) * proj(0)
    u_ref[...] = u
    ubuf[0:8, :] = jnp.zeros((8, conv_dim), F32)
    ubuf[8:8 + tm, :] = u
    um2 = jnp.where(sel2_ref[...] > 0.5, pre2_ref[...], ubuf[6:6 + tm, :])
    um1 = jnp.where(sel1_ref[...] > 0.5, pre1_ref[...], ubuf[7:7 + tm, :])
    cw = cw_ref[...]
    conv = um2 * cw[0:1] + um1 * cw[1:2]
    conv = conv + u * cw[2:3]
    yc_ref[...] = proj(1) * conv
    cos = cos_ref[...]
    sin = sin_ref[...]
    _rope_store(proj(3), cos, sin, HEAD_DIM ** -0.5, (q_ref,))
    _rope_store(proj(4), cos, sin, 1.0, (k_ref,))
    v_ref[...] = proj(5)


def _inproj_main(x, w, wkt, cw, cos, sin, cost, sint, upre, kmt, cache_out, *, seq, tm, layer, depth):
    rows, d_model = x.shape
    conv_dim = cw.shape[2]
    attn_dim = wkt.shape[1]
    tiles_per_seq = seq // tm
    n_seq = rows // seq
    seq_of = lambda i: i // tiles_per_seq
    tile_of = lambda i: i % tiles_per_seq
    row_blk = lambda width: pl.BlockSpec((tm, width), lambda i: (i, 0))
    tab_blk = pl.BlockSpec((tm, LANES), lambda i: (tile_of(i), 0))
    tabt_blk = pl.BlockSpec((HEAD_DIM, tm), lambda i: (0, tile_of(i)))
    kt_blk = pl.BlockSpec((None, attn_dim, tm), lambda i: (seq_of(i), 0, tile_of(i)))
    kt_out_blk = pl.BlockSpec((None, None, attn_dim, tm), lambda i: (layer, seq_of(i), 0, tile_of(i)))
    v_out_blk = pl.BlockSpec((pl.Element(1), pl.Element(1), pl.Element(tm * N_HEADS), pl.Element(V_DIM)),
                             lambda i: (layer, seq_of(i),
                                        pl.multiple_of((N_META + tile_of(i) * tm) * N_HEADS, 8), 0))
    out_shape = (
        jax.ShapeDtypeStruct((rows, conv_dim), BF16),
        jax.ShapeDtypeStruct((rows, attn_dim), BF16),
        jax.ShapeDtypeStruct((depth, n_seq, attn_dim, N_META + seq), F32),
        jax.ShapeDtypeStruct((depth, n_seq, (N_META + seq) * N_HEADS, V_DIM), F32),
        jax.ShapeDtypeStruct((n_seq, attn_dim, seq), BF16),
        jax.ShapeDtypeStruct((rows, attn_dim), BF16),
        jax.ShapeDtypeStruct((n_seq, CONV_W - 1, conv_dim), F32),
        jax.ShapeDtypeStruct((n_seq, attn_dim, LANES), F32),
    )
    args = [x, w, wkt, cw, cos, sin, cost, sint, upre, kmt]
    in_specs = [row_blk(d_model), _of_layer(w, layer), _of_layer(wkt, layer), _of_layer(cw, layer),
                tab_blk, tab_blk, tabt_blk, tabt_blk, _full(upre.shape), _full(kmt.shape)]
    aliases = {}
    if cache_out is not None:
        aliases = {len(args): 2, len(args) + 1: 3}
        args += list(cache_out)
        in_specs += [pl.BlockSpec(memory_space=pl.ANY)] * 2
    return pl.pallas_call(
        functools.partial(_inproj_main_kernel, tm=tm, tiles_per_seq=tiles_per_seq, conv_dim=conv_dim),
        grid=(rows // tm,),
        in_specs=in_specs,
        out_specs=(row_blk(conv_dim), row_blk(attn_dim), kt_out_blk, v_out_blk, kt_blk, row_blk(attn_dim),
                   pl.BlockSpec((None, CONV_W - 1, conv_dim), lambda i: (seq_of(i), 0, 0)),
                   pl.BlockSpec((None, attn_dim, LANES), lambda i: (seq_of(i), 0, 0))),
        out_shape=out_shape,
        input_output_aliases=aliases,
        scratch_shapes=[pltpu.VMEM((tm + 8, conv_dim), F32)],
        compiler_params=_params("arbitrary"),
        name="inproj_main",
    )(*args)


def _inproj_side(x, w, cw, cos, sin, pre1, pre2, sel1, sel2, *, layer):
    rows, _ = x.shape
    conv_dim = cw.shape[2]
    args = (x, w, cw, cos, sin, pre1, pre2, sel1, sel2)
    out = jax.ShapeDtypeStruct((rows, conv_dim), F32)
    return pl.pallas_call(
        functools.partial(_inproj_side_kernel, tm=rows, conv_dim=conv_dim),
        grid=(1,),
        in_specs=[_of_layer(a, layer) if a is w or a is cw else _full(a.shape) for a in args],
        out_specs=tuple(_full(out.shape) for _ in range(5)),
        out_shape=(out,) * 5,
        scratch_shapes=[pltpu.VMEM((rows + 8, conv_dim), F32)],
        compiler_params=_params("arbitrary"),
        name="inproj_side",
    )(*args)


def _prompt_attn_kernel(qi_ref, kj_ref, q_ref, kt_ref, v_ref, kmt_ref, vm_ref,
                        lq1, lk1, lq2, lk2, sg_ref, o_ref, m_sc, acc_sc, *, tq, tk, lam_init):
    step = pl.program_id(1)
    i = qi_ref[step]
    j = kj_ref[step]

    def head_slices(h):
        return slice(h * HEAD_DIM, (h + 1) * HEAD_DIM), slice((h // 2) * V_DIM, (h // 2 + 1) * V_DIM)

    def pv(p, v_blk):
        ones = jnp.ones((v_blk.shape[0], LANES), BF16)
        return jnp.dot(p.astype(BF16), jnp.concatenate([v_blk, ones], axis=1), preferred_element_type=F32)

    @pl.when(j == 0)
    def _():
        col = lax.broadcasted_iota(jnp.int32, (tq, LANES), 1)
        for h in range(N_SUB):
            qk, vv = head_slices(h)
            s = jnp.dot(q_ref[:, qk], kmt_ref[qk, :], preferred_element_type=F32)
            s = jnp.where(col < N_META, s, NEG_INF)
            m = jnp.max(s, axis=1, keepdims=True)
            m_sc[h] = jnp.broadcast_to(m, (tq, LANES))
            acc_sc[h] = pv(jnp.exp2(s - m), vm_ref[:, vv])

    def process(rows, triangular):
        n_rows = rows.stop - rows.start
        if triangular:
            row = lax.broadcasted_iota(jnp.int32, (n_rows, tk), 0)
            col = lax.broadcasted_iota(jnp.int32, (n_rows, tk), 1)
            visible = col <= row
        for h in range(N_SUB):
            qk, vv = head_slices(h)
            s = jnp.dot(q_ref[rows, qk], kt_ref[qk, :], preferred_element_type=F32)
            if triangular:
                s = jnp.where(visible, s, NEG_INF)
            m_prev = m_sc[h, rows]
            m_new = jnp.maximum(m_prev, jnp.max(s, axis=1, keepdims=True))
            alpha = jnp.exp2(m_prev - m_new)
            p = jnp.exp2(s - _lanes(m_new, tk))
            m_sc[h, rows] = m_new
            acc_sc[h, rows] = acc_sc[h, rows] * _lanes(alpha, V_DIM + LANES) + pv(p, v_ref[:, vv])

    per_q = tq // tk
    j_local = j - i * per_q

    @pl.when(j_local < 0)
    def _():
        process(slice(0, tq), False)

    for jl in range(per_q):
        @pl.when(j_local == jl)
        def _(jl=jl):
            process(slice(jl * tk, (jl + 1) * tk), True)
            if jl + 1 < per_q:
                process(slice((jl + 1) * tk, tq), False)

    @pl.when(j_local == per_q - 1)
    def _():
        lam = _lam(lq1, lk1, lq2, lk2, lam_init)
        g = sg_ref[...]
        for hv in range(N_HEADS):
            a1 = acc_sc[2 * hv]
            a2 = acc_sc[2 * hv + 1]
            o1 = a1[:, :V_DIM] / a1[:, V_DIM:]
            o2 = a2[:, :V_DIM] / a2[:, V_DIM:]
            o_ref[:, hv * V_DIM:(hv + 1) * V_DIM] = _diff_norm(o1, o2, lam, g, lam_init).astype(o_ref.dtype)


def _prompt_attention(q, kt, v, kt_meta, v_meta, lam_vecs, sub_g, *, seq, tq, tk, layer, lam_init):
    rows, attn_dim = q.shape
    n_seq = rows // seq
    nq, nk = seq // tq, seq // tk
    per_q = tq // tk
    pairs = [(i, j) for i in range(nq) for j in range((i + 1) * per_q)]
    qi = jnp.asarray([p[0] for p in pairs], jnp.int32)
    kj = jnp.asarray([p[1] for p in pairs], jnp.int32)
    q_blk = pl.BlockSpec((tq, attn_dim), lambda b, s, qi, kj: (b * nq + qi[s], 0))
    v_blk = pl.BlockSpec((tk, attn_dim), lambda b, s, qi, kj: (b * nk + kj[s], 0))
    kt_blk = pl.BlockSpec((None, attn_dim, tk), lambda b, s, qi, kj: (b, 0, kj[s]))
    small = [_full(kt_meta.shape), _full(v_meta.shape)] + [_of_layer(a, layer) for a in (*lam_vecs, sub_g)]
    grid_spec = pltpu.PrefetchScalarGridSpec(
        num_scalar_prefetch=2,
        grid=(n_seq, len(pairs)),
        in_specs=[q_blk, kt_blk, v_blk] + small,
        out_specs=q_blk,
        scratch_shapes=[pltpu.VMEM((N_SUB, tq, LANES), F32), pltpu.VMEM((N_SUB, tq, V_DIM + LANES), F32)],
    )
    return pl.pallas_call(
        functools.partial(_prompt_attn_kernel, tq=tq, tk=tk, lam_init=lam_init),
        grid_spec=grid_spec,
        out_shape=jax.ShapeDtypeStruct((rows, attn_dim), BF16),
        compiler_params=_params("parallel", "arbitrary"),
        name="prompt_attn",
    )(qi, kj, q, kt, v, kt_meta, v_meta, *lam_vecs, sub_g)


def _side_attn_kernel(pt_ref, q_ref, kn_ref, vn_ref, *rest, n, pages_per_step, lam_init):
    del pt_ref
    k_refs = rest[:pages_per_step]
    v_refs = rest[pages_per_step:2 * pages_per_step]
    lq1, lk1, lq2, lk2, sg_ref, o_ref, qbd_sc, m_sc, l_sc, acc_sc = rest[2 * pages_per_step:]
    g = pl.program_id(1)
    rows = N_SUB * n
    attn_dim = q_ref.shape[1]

    @pl.when(g == 0)
    def _():
        qt = jnp.concatenate([q_ref[...]] * N_SUB, axis=0)
        row = lax.broadcasted_iota(jnp.int32, (rows, attn_dim), 0)
        lane = lax.broadcasted_iota(jnp.int32, (rows, attn_dim), 1)
        own = (row // n) == (lane // HEAD_DIM)
        qbd_sc[...] = jnp.where(own, qt, 0.0).astype(qbd_sc.dtype)
        m_sc[...] = jnp.full(m_sc.shape, NEG_INF, F32)
        l_sc[...] = jnp.zeros(l_sc.shape, F32)
        acc_sc[...] = jnp.zeros(acc_sc.shape, F32)

    def update(s, value_block):
        p, alpha, m_new, l_new = _online_softmax(s, m_sc[...], l_sc[...])
        m_sc[...] = m_new
        l_sc[...] = l_new
        for hv in range(N_HEADS):
            pair = slice(2 * hv * n, 2 * (hv + 1) * n)
            acc = acc_sc[pair, :] * alpha[pair]
            for c in range(s.shape[1] // LANES):
                acc = acc + jnp.dot(p[pair, c * LANES:(c + 1) * LANES].astype(BF16),
                                    value_block(hv, c).astype(BF16), preferred_element_type=F32)
            acc_sc[pair, :] = acc

    if pages_per_step:
        qbd = qbd_sc[...]
        s = jnp.concatenate([jnp.dot(qbd, k_ref[...].astype(BF16), preferred_element_type=F32)
                             for k_ref in k_refs], axis=1)
        update(s, lambda hv, c: v_refs[c][pl.ds(hv, LANES, stride=N_HEADS), :])

    @pl.when(g == pl.num_programs(1) - 1)
    def _():
        pad = jnp.zeros((LANES - n, attn_dim), F32)
        kn = jnp.concatenate([kn_ref[...], pad], axis=0).astype(BF16)
        vn = jnp.concatenate([vn_ref[...], pad], axis=0)
        row = lax.broadcasted_iota(jnp.int32, (rows, LANES), 0)
        col = lax.broadcasted_iota(jnp.int32, (rows, LANES), 1)
        s = jnp.where(col <= (row % n), _dot_nt(qbd_sc[...], kn), NEG_INF)
        update(s, lambda hv, c: vn[:, hv * V_DIM:(hv + 1) * V_DIM])
        lam = _lam(lq1, lk1, lq2, lk2, lam_init)
        gain = sg_ref[...]
        out = acc_sc[...] / l_sc[...]
        for hv in range(N_HEADS):
            r0 = 2 * hv * n
            o_ref[:, hv * V_DIM:(hv + 1) * V_DIM] = _diff_norm(out[r0:r0 + n], out[r0 + n:r0 + 2 * n],
                                                               lam, gain, lam_init)


def _side_attention(q, k_new, v_new, lam_vecs, sub_g, *, n, first_row, rows, layer, lam_init,
                    cache=None, pages_per_step=0):
    attn_dim = q.shape[1]
    n_seq = rows // n
    first_blk = first_row // n
    tok_blk = pl.BlockSpec((n, attn_dim), lambda b, g, pt: (first_blk + b, 0))
    out_blk = pl.BlockSpec((n, attn_dim), lambda b, g, pt: (b, 0))
    small = [_of_layer(a, layer) for a in (*lam_vecs, sub_g)]
    if cache is None:
        page_table = jnp.zeros((1,), jnp.int32)
        steps, kv_specs, kv_args = 1, [], []
    else:
        cache_k, cache_v, page_table = cache
        n_pages = page_table.shape[1]
        steps = n_pages // pages_per_step
        page_table = page_table.reshape(-1)

        def page_blk(jj):
            return pl.BlockSpec(
                (None, None) + cache_k.shape[2:],
                lambda b, g, pt: (layer, pt[b * n_pages + g * pages_per_step + jj], 0, 0))

        kv_specs = [page_blk(jj) for jj in range(pages_per_step)] * 2
        kv_args = [cache_k] * pages_per_step + [cache_v] * pages_per_step
    grid_spec = pltpu.PrefetchScalarGridSpec(
        num_scalar_prefetch=1,
        grid=(n_seq, steps),
        in_specs=[tok_blk, tok_blk, tok_blk] + kv_specs + small,
        out_specs=out_blk,
        scratch_shapes=[pltpu.VMEM((N_SUB * n, attn_dim), BF16), pltpu.VMEM((N_SUB * n, LANES), F32),
                        pltpu.VMEM((N_SUB * n, LANES), F32), pltpu.VMEM((N_SUB * n, V_DIM), F32)],
    )
    return pl.pallas_call(
        functools.partial(_side_attn_kernel, n=n, pages_per_step=pages_per_step if cache else 0,
                          lam_init=lam_init),
        grid_spec=grid_spec,
        out_shape=jax.ShapeDtypeStruct((rows, attn_dim), F32),
        compiler_params=_params("parallel", "arbitrary"),
        name="sample_attn" if cache else "meta_attn",
    )(page_table, q, k_new, v_new, *kv_args, *lam_vecs, sub_g)


def _post_kernel(yc_ref, ya_ref, x_ref, wo_ref, wgu_ref, wd_ref, g1, b1, g2, b2,
                 o_ref, x1_sc, x1b_sc, acc_sc, *, alpha, conv_dim):
    d_ff = wd_ref.shape[0]
    mix = jnp.dot(yc_ref[...].astype(BF16), wo_ref[0:conv_dim, :], preferred_element_type=F32)
    mix = mix + jnp.dot(ya_ref[...].astype(BF16), wo_ref[conv_dim:, :], preferred_element_type=F32)
    x1 = _layernorm(alpha * x_ref[...] + mix, g1[...], b1[...])
    x1_sc[...] = x1
    x1b_sc[...] = x1.astype(BF16)
    acc_sc[...] = jnp.zeros(acc_sc.shape, F32)

    for c0 in range(0, d_ff, FF_CHUNK):
        x1b = x1b_sc[...]
        gate = jnp.dot(x1b, wgu_ref[:, c0:c0 + FF_CHUNK], preferred_element_type=F32)
        up = jnp.dot(x1b, wgu_ref[:, d_ff + c0:d_ff + c0 + FF_CHUNK], preferred_element_type=F32)
        hidden = (gate / (1.0 + jnp.exp(-gate))) * up
        acc_sc[...] += jnp.dot(hidden.astype(BF16), wd_ref[c0:c0 + FF_CHUNK, :], preferred_element_type=F32)
    o_ref[...] = _layernorm(alpha * x1_sc[...] + acc_sc[...], g2[...], b2[...])


def _post_attention(yc, ya, x, wo, wgu, wd, g1, b1, g2, b2, *, tm, alpha, layer):
    rows, d_model = x.shape
    conv_dim = yc.shape[1]
    assert wd.shape[1] % FF_CHUNK == 0
    row_blk = lambda width: pl.BlockSpec((tm, width), lambda i: (i, 0))
    consts = (wo, wgu, wd, g1, b1, g2, b2)
    return pl.pallas_call(
        functools.partial(_post_kernel, alpha=alpha, conv_dim=conv_dim),
        grid=(rows // tm,),
        in_specs=[row_blk(conv_dim), row_blk(ya.shape[1]), row_blk(d_model)] + [_of_layer(a, layer) for a in consts],
        out_specs=row_blk(d_model),
        out_shape=jax.ShapeDtypeStruct((rows, d_model), F32),
        scratch_shapes=[pltpu.VMEM((tm, d_model), F32), pltpu.VMEM((tm, d_model), BF16),
                        pltpu.VMEM((tm, d_model), F32)],
        compiler_params=_params("parallel"),
        name="post_main" if rows > tm else "post_side",
    )(yc, ya, x, *consts)


def _rope_tables(pos):
    inv = 1.0 / (ROPE_THETA ** (jnp.arange(0, HEAD_DIM, 2, dtype=F32) / HEAD_DIM))
    ang = pos.astype(F32)[:, None] * inv[None, :]
    cos = jnp.cos(ang)
    sin = jnp.sin(ang)
    return jnp.concatenate([cos] * 4, axis=-1), jnp.concatenate([-sin, sin] * 2, axis=-1)


def _row_tile(n, target):
    t = min(n, target)
    while n % t:
        t //= 2
    return t


def kernel(x_prompt, x_sample, cache_k, cache_v, state_conv, page_table, meta_tokens, w_in, conv_w, w_out, lambda_q1, lambda_k1, lambda_q2, lambda_k2, subln_g, ln1_g, ln1_b, ln2_g, ln2_b, w_gate_up, w_down):
    batch, seq, d_model = x_prompt.shape
    dec_batch, dec_seq, _ = x_sample.shape
    depth = w_in.shape[0]
    conv_dim = conv_w.shape[2]
    attn_dim = N_SUB * HEAD_DIM
    d_ff = w_down.shape[1]
    n_pool, page_size = cache_k.shape[1], cache_k.shape[2]
    past_len = page_table.shape[1] * page_size
    alpha = (2 * depth) ** 0.25
    n_sample = dec_batch * dec_seq
    n_side = n_sample + N_META
    tm = _row_tile(seq, 512)
    tq = _row_tile(seq, ATTN_TQ)
    tk = _row_tile(tq, ATTN_TK)
    pages_per_step = _row_tile(page_table.shape[1], 16)
    assert page_size == LANES and attn_dim == conv_dim
    assert n_sample % N_META == 0

    w_in_b = w_in.astype(BF16)
    k_col = 3 * conv_dim + attn_dim
    w_kt = w_in_b[:, :, k_col:k_col + attn_dim].transpose(0, 2, 1)
    w_out_b = w_out.astype(BF16)
    w_gate_up_b = w_gate_up.astype(BF16)
    w_down_b = w_down.astype(BF16)
    per_layer_row = lambda a: a[:, None, :]
    lam_vecs = tuple(per_layer_row(a) for a in (lambda_q1, lambda_k1, lambda_q2, lambda_k2))
    sub_g = per_layer_row(subln_g)
    norms = tuple(per_layer_row(a) for a in (ln1_g, ln1_b, ln2_g, ln2_b))
    cache_kt = cache_k.transpose(0, 1, 3, 4, 2).reshape(depth, n_pool, attn_dim, page_size)
    cache_vi = cache_v.reshape(depth, n_pool, page_size * N_HEADS, V_DIM)

    cos_m, sin_m = _rope_tables(N_META + jnp.arange(seq, dtype=jnp.int32))
    cost_m, sint_m = cos_m[:, :HEAD_DIM].T, sin_m[:, :HEAD_DIM].T
    pos_side = jnp.concatenate([jnp.tile(past_len + jnp.arange(dec_seq, dtype=jnp.int32), dec_batch),
                                jnp.arange(N_META, dtype=jnp.int32)])
    cos_s, sin_s = _rope_tables(pos_side)

    j_in_seq = jnp.concatenate([jnp.tile(jnp.arange(dec_seq, dtype=jnp.int32), dec_batch),
                                jnp.arange(N_META, dtype=jnp.int32)])
    sel1 = (j_in_seq < 1).astype(F32)[:, None]
    sel2 = (j_in_seq < 2).astype(F32)[:, None]

    def prefixes(state):
        z = jnp.zeros((dec_batch, dec_seq, conv_dim), F32)
        p1 = z.at[:, 0].set(state[:, 1]).reshape(n_sample, conv_dim)
        p2 = z.at[:, 0].set(state[:, 0]).at[:, 1].set(state[:, 1]).reshape(n_sample, conv_dim)
        zm = jnp.zeros((N_META, conv_dim), F32)
        return jnp.concatenate([p1, zm]), jnp.concatenate([p2, zm])

    x_main = x_prompt.reshape(batch * seq, d_model)
    x_side = jnp.concatenate([x_sample.reshape(n_sample, d_model), meta_tokens.astype(F32)])

    outs = {name: [] for name in ("cp", "ks", "vs", "cs")}
    prompt_cache, kt_tails, meta_v = None, [], []
    for l in range(depth):
        lam_init = 0.8 - 0.6 * math.exp(-0.3 * l)

        pre1, pre2 = prefixes(state_conv[l])
        yc_s, q_s, k_s, v_s, u_s = _inproj_side(x_side, w_in_b, conv_w, cos_s, sin_s, pre1, pre2, sel1, sel2,
                                                layer=l)
        k_meta, v_meta, u_meta = k_s[n_sample:], v_s[n_sample:], u_s[n_sample:]

        kmt = jnp.concatenate([k_meta, jnp.zeros((LANES - N_META, attn_dim), F32)]).T
        yc_m, q_m, kt_cache, v_cache, ktb_m, vb_m, cstate, kt_tail = _inproj_main(
            x_main, w_in_b, w_kt, conv_w, cos_m, sin_m, cost_m, sint_m,
            u_meta[N_META - (CONV_W - 1):], kmt, prompt_cache, seq=seq, tm=tm, layer=l, depth=depth)
        prompt_cache = (kt_cache, v_cache)
        kt_tails.append(kt_tail[:, :, :N_META])
        meta_v.append(v_meta.reshape(N_META * N_HEADS, V_DIM))

        pad = jnp.zeros((LANES - N_META, attn_dim), BF16)
        ya_m = _prompt_attention(q_m, ktb_m, vb_m, kmt.astype(BF16),
                                 jnp.concatenate([v_meta.astype(BF16), pad]),
                                 lam_vecs, sub_g, seq=seq, tq=tq, tk=tk, layer=l, lam_init=lam_init)
        side_attend = functools.partial(_side_attention, q_s, k_s, v_s, lam_vecs, sub_g, layer=l, lam_init=lam_init)
        ya_sample = side_attend(n=dec_seq, first_row=0, rows=n_sample,
                                cache=(cache_kt, cache_vi, page_table), pages_per_step=pages_per_step)
        ya_meta = side_attend(n=N_META, first_row=n_sample, rows=N_META)
        ya_s = jnp.concatenate([ya_sample, ya_meta])

        post = functools.partial(_post_attention, wo=w_out_b, wgu=w_gate_up_b, wd=w_down_b, g1=norms[0],
                                 b1=norms[1], g2=norms[2], b2=norms[3], alpha=alpha, layer=l)
        x_main = post(yc_m, ya_m, x_main, tm=tm)
        x_side = post(yc_s, ya_s, x_side, tm=n_side)

        outs["cp"].append(cstate)
        outs["ks"].append(k_s[:n_sample].reshape(dec_batch, dec_seq, N_SUB, HEAD_DIM))
        outs["vs"].append(v_s[:n_sample].reshape(dec_batch, dec_seq, N_HEADS, V_DIM))
        outs["cs"].append(u_s[:n_sample].reshape(dec_batch, dec_seq, conv_dim)[:, dec_seq - (CONV_W - 1):])

    kt_cache, v_cache = prompt_cache
    new_k_prompt = lax.dynamic_update_slice(kt_cache, jnp.stack(kt_tails), (0, 0, 0, seq)).reshape(
        depth, batch, N_SUB, HEAD_DIM, N_META + seq).transpose(0, 1, 4, 2, 3)
    meta_v_b = jnp.broadcast_to(jnp.stack(meta_v)[:, None], (depth, batch, N_META * N_HEADS, V_DIM))
    new_v_prompt = lax.dynamic_update_slice(v_cache, meta_v_b, (0, 0, 0, 0)).reshape(
        depth, batch, N_META + seq, N_HEADS, V_DIM)

    y_prompt = x_main.reshape(batch, seq, d_model)
    y_sample = x_side[:n_sample].reshape(dec_batch, dec_seq, d_model)
    return (y_prompt, y_sample, new_k_prompt, new_v_prompt, jnp.stack(outs["cp"]),
            jnp.stack(outs["ks"]), jnp.stack(outs["vs"]), jnp.stack(outs["cs"]))
```

```python
import functools
import math

import jax
import jax.numpy as jnp
from jax import lax
from jax.experimental import pallas as pl
from jax.experimental.pallas import tpu as pltpu

F32 = jnp.float32
BF16 = jnp.bfloat16

N_META = 16
CONV_W = 3
N_HEADS = 4
N_SUB = 2 * N_HEADS
HEAD_DIM = 64
V_DIM = 2 * HEAD_DIM
ROPE_THETA = 10000.0
LN_EPS = 1e-5
NEG_INF = -1e30
LOG2_E = math.log2(math.e)
LANES = 128
FF_CHUNK = 256
ATTN_TQ = 1024
ATTN_TK = 512
VMEM_LIMIT = 56 * 1024 * 1024


def _params(*sem):
    return pltpu.CompilerParams(dimension_semantics=sem, vmem_limit_bytes=VMEM_LIMIT)


def _full(shape):
    return pl.BlockSpec(shape, lambda *_: (0,) * len(shape))


def _of_layer(stacked, layer):
    tail = stacked.shape[1:]
    return pl.BlockSpec((None,) + tail, lambda *_: (layer,) + (0,) * len(tail))


def _rope128(x, cos, sin_signed):
    lane = lax.broadcasted_iota(jnp.int32, x.shape, 1)
    swapped = jnp.where((lane & (HEAD_DIM // 2)) == 0,
                        pltpu.roll(x, LANES - HEAD_DIM // 2, 1), pltpu.roll(x, HEAD_DIM // 2, 1))
    return x * cos + swapped * sin_signed


def _layernorm(h, g, b):
    mu = jnp.mean(h, axis=-1, keepdims=True)
    d = h - mu
    var = jnp.mean(d * d, axis=-1, keepdims=True)
    return d * lax.rsqrt(var + LN_EPS) * g + b


def _lam(lq1, lk1, lq2, lk2, lam_init):
    a = jnp.sum(lq1[...] * lk1[...], axis=1, keepdims=True)
    b = jnp.sum(lq2[...] * lk2[...], axis=1, keepdims=True)
    return jnp.exp(a) - jnp.exp(b) + lam_init


def _diff_norm(o1, o2, lam, g, lam_init):
    o = o1 - lam * o2
    o = o * lax.rsqrt(jnp.mean(o * o, axis=-1, keepdims=True) + LN_EPS)
    return o * g * (1.0 - lam_init)


def _lanes(x, width):
    reps = width // LANES
    return x if reps == 1 else jnp.concatenate([x] * reps, axis=1)


def _online_softmax(s, m_prev, l_prev):
    m_new = jnp.maximum(m_prev, jnp.max(s, axis=1, keepdims=True))
    alpha = jnp.exp(m_prev - m_new)
    p = jnp.exp(s - _lanes(m_new, s.shape[1]))
    l_new = alpha * l_prev + jnp.sum(p, axis=1, keepdims=True)
    return p, alpha, m_new, l_new


def _dot_nt(a, b):
    return lax.dot_general(a, b, (((1,), (1,)), ((), ())), preferred_element_type=F32)


def _project(x_ref, w_ref, conv_dim):
    xb = x_ref[...].astype(BF16)

    def proj(seg):
        return jnp.dot(xb, w_ref[:, seg * conv_dim:(seg + 1) * conv_dim], preferred_element_type=F32)

    return proj


def _rope_store(val, cos, sin, scale, refs):
    for c in range(val.shape[1] // LANES):
        sl = slice(c * LANES, (c + 1) * LANES)
        r = _rope128(val[:, sl], cos, sin)
        if scale != 1.0:
            r = r * scale
        for ref in refs:
            ref[:, sl] = r.astype(ref.dtype)


def _inproj_main_kernel(x_ref, w_ref, wkt_ref, cw_ref, cos_ref, sin_ref, cost_ref, sint_ref, upre_ref,
                        kmt_ref, *rest, tm, tiles_per_seq, conv_dim):
    yc_ref, q_ref, kt_ref, v_ref, ktb_ref, vb_ref, cs_ref, ktail_ref, ubuf = rest[-9:]

    @pl.when(pl.program_id(0) % tiles_per_seq == 0)
    def _():
        ubuf[6:8, :] = upre_ref[...]
        ktail_ref[...] = kmt_ref[...]

    proj = _project(x_ref, w_ref, conv_dim)
    u = proj(2) * proj(0)
    ubuf[8:8 + tm, :] = u
    cw = cw_ref[...]
    conv = ubuf[6:6 + tm, :] * cw[0:1] + ubuf[7:7 + tm, :] * cw[1:2]
    conv = conv + ubuf[8:8 + tm, :] * cw[2:3]
    yc_ref[...] = (proj(1) * conv).astype(yc_ref.dtype)
    tail = ubuf[tm + 6:tm + 8, :]
    cs_ref[...] = tail
    ubuf[6:8, :] = tail

    cos = cos_ref[...]
    sin = sin_ref[...]
    _rope_store(proj(3), cos, sin, HEAD_DIM ** -0.5 * LOG2_E, (q_ref,))

    kt = _dot_nt(wkt_ref[...], x_ref[...].astype(BF16))
    cost = cost_ref[...]
    sint = sint_ref[...]
    half = HEAD_DIM // 2
    lane = lax.broadcasted_iota(jnp.int32, (HEAD_DIM, LANES), 1)
    for h in range(N_SUB):
        hd = slice(h * HEAD_DIM, (h + 1) * HEAD_DIM)
        blk = kt[hd, :]
        swapped = jnp.concatenate([blk[half:], blk[:half]], axis=0)
        r = blk * cost + swapped * sint
        ktb_ref[hd, :] = r.astype(ktb_ref.dtype)
        shifted = pltpu.roll(r, N_META, 1)
        kt_ref[hd, :LANES] = jnp.where(lane < N_META, ktail_ref[hd, :], shifted[:, :LANES])
        kt_ref[hd, LANES:] = shifted[:, LANES:]
        ktail_ref[hd, :] = shifted[:, :LANES]

    vv = proj(5)
    vb_ref[...] = vv.astype(vb_ref.dtype)
    for hv in range(N_HEADS):
        v_ref[0, 0, pl.ds(hv, tm, stride=N_HEADS), :] = vv[:, hv * V_DIM:(hv + 1) * V_DIM]


def _inproj_side_kernel(x_ref, w_ref, cw_ref, cos_ref, sin_ref, pre1_ref, pre2_ref, sel1_ref, sel2_ref,
                        yc_ref, q_ref, k_ref, v_ref, u_ref, ubuf, *, tm, conv_dim):
    proj = _project(x_ref, w_ref, conv_dim)
    u = proj(2) * proj(0)
    u_ref[...] = u
    ubuf[0:8, :] = jnp.zeros((8, conv_dim), F32)
    ubuf[8:8 + tm, :] = u
    um2 = jnp.where(sel2_ref[...] > 0.5, pre2_ref[...], ubuf[6:6 + tm, :])
    um1 = jnp.where(sel1_ref[...] > 0.5, pre1_ref[...], ubuf[7:7 + tm, :])
    cw = cw_ref[...]
    conv = um2 * cw[0:1] + um1 * cw[1:2]
    conv = conv + u * cw[2:3]
    yc_ref[...] = proj(1) * conv
    cos = cos_ref[...]
    sin = sin_ref[...]
    _rope_store(proj(3), cos, sin, HEAD_DIM ** -0.5, (q_ref,))
    _rope_store(proj(4), cos, sin, 1.0, (k_ref,))
    v_ref[...] = proj(5)


def _inproj_main(x, w, wkt, cw, cos, sin, cost, sint, upre, kmt, cache_out, *, seq, tm, layer, depth):
    rows, d_model = x.shape
    conv_dim = cw.shape[2]
    attn_dim = wkt.shape[1]
    tiles_per_seq = seq // tm
    n_seq = rows // seq
    seq_of = lambda i: i // tiles_per_seq
    tile_of = lambda i: i % tiles_per_seq
    row_blk = lambda width: pl.BlockSpec((tm, width), lambda i: (i, 0))
    tab_blk = pl.BlockSpec((tm, LANES), lambda i: (tile_of(i), 0))
    tabt_blk = pl.BlockSpec((HEAD_DIM, tm), lambda i: (0, tile_of(i)))
    kt_blk = pl.BlockSpec((None, attn_dim, tm), lambda i: (seq_of(i), 0, tile_of(i)))
    kt_out_blk = pl.BlockSpec((None, None, attn_dim, tm), lambda i: (layer, seq_of(i), 0, tile_of(i)))
    v_out_blk = pl.BlockSpec((pl.Element(1), pl.Element(1), pl.Element(tm * N_HEADS), pl.Element(V_DIM)),
                             lambda i: (layer, seq_of(i),
                                        pl.multiple_of((N_META + tile_of(i) * tm) * N_HEADS, 8), 0))
    out_shape = (
        jax.ShapeDtypeStruct((rows, conv_dim), BF16),
        jax.ShapeDtypeStruct((rows, attn_dim), BF16),
        jax.ShapeDtypeStruct((depth, n_seq, attn_dim, N_META + seq), F32),
        jax.ShapeDtypeStruct((depth, n_seq, (N_META + seq) * N_HEADS, V_DIM), F32),
        jax.ShapeDtypeStruct((n_seq, attn_dim, seq), BF16),
        jax.ShapeDtypeStruct((rows, attn_dim), BF16),
        jax.ShapeDtypeStruct((n_seq, CONV_W - 1, conv_dim), F32),
        jax.ShapeDtypeStruct((n_seq, attn_dim, LANES), F32),
    )
    args = [x, w, wkt, cw, cos, sin, cost, sint, upre, kmt]
    in_specs = [row_blk(d_model), _of_layer(w, layer), _of_layer(wkt, layer), _of_layer(cw, layer),
                tab_blk, tab_blk, tabt_blk, tabt_blk, _full(upre.shape), _full(kmt.shape)]
    aliases = {}
    if cache_out is not None:
        aliases = {len(args): 2, len(args) + 1: 3}
        args += list(cache_out)
        in_specs += [pl.BlockSpec(memory_space=pl.ANY)] * 2
    return pl.pallas_call(
        functools.partial(_inproj_main_kernel, tm=tm, tiles_per_seq=tiles_per_seq, conv_dim=conv_dim),
        grid=(rows // tm,),
        in_specs=in_specs,
        out_specs=(row_blk(conv_dim), row_blk(attn_dim), kt_out_blk, v_out_blk, kt_blk, row_blk(attn_dim),
                   pl.BlockSpec((None, CONV_W - 1, conv_dim), lambda i: (seq_of(i), 0, 0)),
                   pl.BlockSpec((None, attn_dim, LANES), lambda i: (seq_of(i), 0, 0))),
        out_shape=out_shape,
        input_output_aliases=aliases,
        scratch_shapes=[pltpu.VMEM((tm + 8, conv_dim), F32)],
        compiler_params=_params("arbitrary"),
        name="inproj_main",
    )(*args)


def _inproj_side(x, w, cw, cos, sin, pre1, pre2, sel1, sel2, *, layer):
    rows, _ = x.shape
    conv_dim = cw.shape[2]
    args = (x, w, cw, cos, sin, pre1, pre2, sel1, sel2)
    out = jax.ShapeDtypeStruct((rows, conv_dim), F32)
    return pl.pallas_call(
        functools.partial(_inproj_side_kernel, tm=rows, conv_dim=conv_dim),
        grid=(1,),
        in_specs=[_of_layer(a, layer) if a is w or a is cw else _full(a.shape) for a in args],
        out_specs=tuple(_full(out.shape) for _ in range(5)),
        out_shape=(out,) * 5,
        scratch_shapes=[pltpu.VMEM((rows + 8, conv_dim), F32)],
        compiler_params=_params("arbitrary"),
        name="inproj_side",
    )(*args)


def _prompt_attn_kernel(qi_ref, kj_ref, q_ref, kt_ref, v_ref, kmt_ref, vm_ref,
                        lq1, lk1, lq2, lk2, sg_ref, o_ref, m_sc, acc_sc, *, tq, tk, lam_init):
    step = pl.program_id(1)
    i = qi_ref[step]
    j = kj_ref[step]

    def head_slices(h):
        return slice(h * HEAD_DIM, (h + 1) * HEAD_DIM), slice((h // 2) * V_DIM, (h // 2 + 1) * V_DIM)

    def pv(p, v_blk):
        ones = jnp.ones((v_blk.shape[0], LANES), BF16)
        return jnp.dot(p.astype(BF16), jnp.concatenate([v_blk, ones], axis=1), preferred_element_type=F32)

    @pl.when(j == 0)
    def _():
        col = lax.broadcasted_iota(jnp.int32, (tq, LANES), 1)
        for h in range(N_SUB):
            qk, vv = head_slices(h)
            s = jnp.dot(q_ref[:, qk], kmt_ref[qk, :], preferred_element_type=F32)
            s = jnp.where(col < N_META, s, NEG_INF)
            m = jnp.max(s, axis=1, keepdims=True)
            m_sc[h] = jnp.broadcast_to(m, (tq, LANES))
            acc_sc[h] = pv(jnp.exp2(s - m), vm_ref[:, vv])

    def process(rows, triangular):
        n_rows = rows.stop - rows.start
        if triangular:
            row = lax.broadcasted_iota(jnp.int32, (n_rows, tk), 0)
            col = lax.broadcasted_iota(jnp.int32, (n_rows, tk), 1)
            visible = col <= row
        for h in range(N_SUB):
            qk, vv = head_slices(h)
            s = jnp.dot(q_ref[rows, qk], kt_ref[qk, :], preferred_element_type=F32)
            if triangular:
                s = jnp.where(visible, s, NEG_INF)
            m_prev = m_sc[h, rows]
            m_new = jnp.maximum(m_prev, jnp.max(s, axis=1, keepdims=True))
            alpha = jnp.exp2(m_prev - m_new)
            p = jnp.exp2(s - _lanes(m_new, tk))
            m_sc[h, rows] = m_new
            acc_sc[h, rows] = acc_sc[h, rows] * _lanes(alpha, V_DIM + LANES) + pv(p, v_ref[:, vv])

    per_q = tq // tk
    j_local = j - i * per_q

    @pl.when(j_local < 0)
    def _():
        process(slice(0, tq), False)

    for jl in range(per_q):
        @pl.when(j_local == jl)
        def _(jl=jl):
            process(slice(jl * tk, (jl + 1) * tk), True)
            if jl + 1 < per_q:
                process(slice((jl + 1) * tk, tq), False)

    @pl.when(j_local == per_q - 1)
    def _():
        lam = _lam(lq1, lk1, lq2, lk2, lam_init)
        g = sg_ref[...]
        for hv in range(N_HEADS):
            a1 = acc_sc[2 * hv]
            a2 = acc_sc[2 * hv + 1]
            o1 = a1[:, :V_DIM] / a1[:, V_DIM:]
            o2 = a2[:, :V_DIM] / a2[:, V_DIM:]
            o_ref[:, hv * V_DIM:(hv + 1) * V_DIM] = _diff_norm(o1, o2, lam, g, lam_init).astype(o_ref.dtype)


def _prompt_attention(q, kt, v, kt_meta, v_meta, lam_vecs, sub_g, *, seq, tq, tk, layer, lam_init):
    rows, attn_dim = q.shape
    n_seq = rows // seq
    nq, nk = seq // tq, seq // tk
    per_q = tq // tk
    pairs = [(i, j) for i in range(nq) for j in range((i + 1) * per_q)]
    qi = jnp.asarray([p[0] for p in pairs], jnp.int32)
    kj = jnp.asarray([p[1] for p in pairs], jnp.int32)
    q_blk = pl.BlockSpec((tq, attn_dim), lambda b, s, qi, kj: (b * nq + qi[s], 0))
    v_blk = pl.BlockSpec((tk, attn_dim), lambda b, s, qi, kj: (b * nk + kj[s], 0))
    kt_blk = pl.BlockSpec((None, attn_dim, tk), lambda b, s, qi, kj: (b, 0, kj[s]))
    small = [_full(kt_meta.shape), _full(v_meta.shape)] + [_of_layer(a, layer) for a in (*lam_vecs, sub_g)]
    grid_spec = pltpu.PrefetchScalarGridSpec(
        num_scalar_prefetch=2,
        grid=(n_seq, len(pairs)),
        in_specs=[q_blk, kt_blk, v_blk] + small,
        out_specs=q_blk,
        scratch_shapes=[pltpu.VMEM((N_SUB, tq, LANES), F32), pltpu.VMEM((N_SUB, tq, V_DIM + LANES), F32)],
    )
    return pl.pallas_call(
        functools.partial(_prompt_attn_kernel, tq=tq, tk=tk, lam_init=lam_init),
        grid_spec=grid_spec,
        out_shape=jax.ShapeDtypeStruct((rows, attn_dim), BF16),
        compiler_params=_params("parallel", "arbitrary"),
        name="prompt_attn",
    )(qi, kj, q, kt, v, kt_meta, v_meta, *lam_vecs, sub_g)


def _side_attn_kernel(pt_ref, q_ref, kn_ref, vn_ref, *rest, n, pages_per_step, lam_init):
    del pt_ref
    k_refs = rest[:pages_per_step]
    v_refs = rest[pages_per_step:2 * pages_per_step]
    lq1, lk1, lq2, lk2, sg_ref, o_ref, qbd_sc, m_sc, l_sc, acc_sc = rest[2 * pages_per_step:]
    g = pl.program_id(1)
    rows = N_SUB * n
    attn_dim = q_ref.shape[1]

    @pl.when(g == 0)
    def _():
        qt = jnp.concatenate([q_ref[...]] * N_SUB, axis=0)
        row = lax.broadcasted_iota(jnp.int32, (rows, attn_dim), 0)
        lane = lax.broadcasted_iota(jnp.int32, (rows, attn_dim), 1)
        own = (row // n) == (lane // HEAD_DIM)
        qbd_sc[...] = jnp.where(own, qt, 0.0).astype(qbd_sc.dtype)
        m_sc[...] = jnp.full(m_sc.shape, NEG_INF, F32)
        l_sc[...] = jnp.zeros(l_sc.shape, F32)
        acc_sc[...] = jnp.zeros(acc_sc.shape, F32)

    def update(s, value_block):
        p, alpha, m_new, l_new = _online_softmax(s, m_sc[...], l_sc[...])
        m_sc[...] = m_new
        l_sc[...] = l_new
        for hv in range(N_HEADS):
            pair = slice(2 * hv * n, 2 * (hv + 1) * n)
            acc = acc_sc[pair, :] * alpha[pair]
            for c in range(s.shape[1] // LANES):
                acc = acc + jnp.dot(p[pair, c * LANES:(c + 1) * LANES].astype(BF16),
                                    value_block(hv, c).astype(BF16), preferred_element_type=F32)
            acc_sc[pair, :] = acc

    if pages_per_step:
        qbd = qbd_sc[...]
        s = jnp.concatenate([jnp.dot(qbd, k_ref[...].astype(BF16), preferred_element_type=F32)
                             for k_ref in k_refs], axis=1)
        update(s, lambda hv, c: v_refs[c][pl.ds(hv, LANES, stride=N_HEADS), :])

    @pl.when(g == pl.num_programs(1) - 1)
    def _():
        pad = jnp.zeros((LANES - n, attn_dim), F32)
        kn = jnp.concatenate([kn_ref[...], pad], axis=0).astype(BF16)
        vn = jnp.concatenate([vn_ref[...], pad], axis=0)
        row = lax.broadcasted_iota(jnp.int32, (rows, LANES), 0)
        col = lax.broadcasted_iota(jnp.int32, (rows, LANES), 1)
        s = jnp.where(col <= (row % n), _dot_nt(qbd_sc[...], kn), NEG_INF)
        update(s, lambda hv, c: vn[:, hv * V_DIM:(hv + 1) * V_DIM])
        lam = _lam(lq1, lk1, lq2, lk2, lam_init)
        gain = sg_ref[...]
        out = acc_sc[...] / l_sc[...]
        for hv in range(N_HEADS):
            r0 = 2 * hv * n
            o_ref[:, hv * V_DIM:(hv + 1) * V_DIM] = _diff_norm(out[r0:r0 + n], out[r0 + n:r0 + 2 * n],
                                                               lam, gain, lam_init)


def _side_attention(q, k_new, v_new, lam_vecs, sub_g, *, n, first_row, rows, layer, lam_init,
                    cache=None, pages_per_step=0):
    attn_dim = q.shape[1]
    n_seq = rows // n
    first_blk = first_row // n
    tok_blk = pl.BlockSpec((n, attn_dim), lambda b, g, pt: (first_blk + b, 0))
    out_blk = pl.BlockSpec((n, attn_dim), lambda b, g, pt: (b, 0))
    small = [_of_layer(a, layer) for a in (*lam_vecs, sub_g)]
    if cache is None:
        page_table = jnp.zeros((1,), jnp.int32)
        steps, kv_specs, kv_args = 1, [], []
    else:
        cache_k, cache_v, page_table = cache
        n_pages = page_table.shape[1]
        steps = n_pages // pages_per_step
        page_table = page_table.reshape(-1)

        def page_blk(jj):
            return pl.BlockSpec(
                (None, None) + cache_k.shape[2:],
                lambda b, g, pt: (layer, pt[b * n_pages + g * pages_per_step + jj], 0, 0))

        kv_specs = [page_blk(jj) for jj in range(pages_per_step)] * 2
        kv_args = [cache_k] * pages_per_step + [cache_v] * pages_per_step
    grid_spec = pltpu.PrefetchScalarGridSpec(
        num_scalar_prefetch=1,
        grid=(n_seq, steps),
        in_specs=[tok_blk, tok_blk, tok_blk] + kv_specs + small,
        out_specs=out_blk,
        scratch_shapes=[pltpu.VMEM((N_SUB * n, attn_dim), BF16), pltpu.VMEM((N_SUB * n, LANES), F32),
                        pltpu.VMEM((N_SUB * n, LANES), F32), pltpu.VMEM((N_SUB * n, V_DIM), F32)],
    )
    return pl.pallas_call(
        functools.partial(_side_attn_kernel, n=n, pages_per_step=pages_per_step if cache else 0,
                          lam_init=lam_init),
        grid_spec=grid_spec,
        out_shape=jax.ShapeDtypeStruct((rows, attn_dim), F32),
        compiler_params=_params("parallel", "arbitrary"),
        name="sample_attn" if cache else "meta_attn",
    )(page_table, q, k_new, v_new, *kv_args, *lam_vecs, sub_g)


def _post_kernel(yc_ref, ya_ref, x_ref, wo_ref, wgu_ref, wd_ref, g1, b1, g2, b2,
                 o_ref, x1_sc, x1b_sc, acc_sc, *, alpha, conv_dim):
    d_ff = wd_ref.shape[0]
    mix = jnp.dot(yc_ref[...].astype(BF16), wo_ref[0:conv_dim, :], preferred_element_type=F32)
    mix = mix + jnp.dot(ya_ref[...].astype(BF16), wo_ref[conv_dim:, :], preferred_element_type=F32)
    x1 = _layernorm(alpha * x_ref[...] + mix, g1[...], b1[...])
    x1_sc[...] = x1
    x1b_sc[...] = x1.astype(BF16)
    acc_sc[...] = jnp.zeros(acc_sc.shape, F32)

    for c0 in range(0, d_ff, FF_CHUNK):
        x1b = x1b_sc[...]
        gate = jnp.dot(x1b, wgu_ref[:, c0:c0 + FF_CHUNK], preferred_element_type=F32)
        up = jnp.dot(x1b, wgu_ref[:, d_ff + c0:d_ff + c0 + FF_CHUNK], preferred_element_type=F32)
        hidden = (gate / (1.0 + jnp.exp(-gate))) * up
        acc_sc[...] += jnp.dot(hidden.astype(BF16), wd_ref[c0:c0 + FF_CHUNK, :], preferred_element_type=F32)
    o_ref[...] = _layernorm(alpha * x1_sc[...] + acc_sc[...], g2[...], b2[...])


def _post_attention(yc, ya, x, wo, wgu, wd, g1, b1, g2, b2, *, tm, alpha, layer):
    rows, d_model = x.shape
    conv_dim = yc.shape[1]
    assert wd.shape[1] % FF_CHUNK == 0
    row_blk = lambda width: pl.BlockSpec((tm, width), lambda i: (i, 0))
    consts = (wo, wgu, wd, g1, b1, g2, b2)
    return pl.pallas_call(
        functools.partial(_post_kernel, alpha=alpha, conv_dim=conv_dim),
        grid=(rows // tm,),
        in_specs=[row_blk(conv_dim), row_blk(ya.shape[1]), row_blk(d_model)] + [_of_layer(a, layer) for a in consts],
        out_specs=row_blk(d_model),
        out_shape=jax.ShapeDtypeStruct((rows, d_model), F32),
        scratch_shapes=[pltpu.VMEM((tm, d_model), F32), pltpu.VMEM((tm, d_model), BF16),
                        pltpu.VMEM((tm, d_model), F32)],
        compiler_params=_params("parallel"),
        name="post_main" if rows > tm else "post_side",
    )(yc, ya, x, *consts)


def _rope_tables(pos):
    inv = 1.0 / (ROPE_THETA ** (jnp.arange(0, HEAD_DIM, 2, dtype=F32) / HEAD_DIM))
    ang = pos.astype(F32)[:, None] * inv[None, :]
    cos = jnp.cos(ang)
    sin = jnp.sin(ang)
    return jnp.concatenate([cos] * 4, axis=-1), jnp.concatenate([-sin, sin] * 2, axis=-1)


def _row_tile(n, target):
    t = min(n, target)
    while n % t:
        t //= 2
    return t


def kernel(x_prompt, x_sample, cache_k, cache_v, state_conv, page_table, meta_tokens, w_in, conv_w, w_out, lambda_q1, lambda_k1, lambda_q2, lambda_k2, subln_g, ln1_g, ln1_b, ln2_g, ln2_b, w_gate_up, w_down):
    batch, seq, d_model = x_prompt.shape
    dec_batch, dec_seq, _ = x_sample.shape
    depth = w_in.shape[0]
    conv_dim = conv_w.shape[2]
    attn_dim = N_SUB * HEAD_DIM
    d_ff = w_down.shape[1]
    n_pool, page_size = cache_k.shape[1], cache_k.shape[2]
    past_len = page_table.shape[1] * page_size
    alpha = (2 * depth) ** 0.25
    n_sample = dec_batch * dec_seq
    n_side = n_sample + N_META
    tm = _row_tile(seq, 512)
    tq = _row_tile(seq, ATTN_TQ)
    tk = _row_tile(tq, ATTN_TK)
    pages_per_step = _row_tile(page_table.shape[1], 16)
    assert page_size == LANES and attn_dim == conv_dim
    assert n_sample % N_META == 0

    w_in_b = w_in.astype(BF16)
    k_col = 3 * conv_dim + attn_dim
    w_kt = w_in_b[:, :, k_col:k_col + attn_dim].transpose(0, 2, 1)
    w_out_b = w_out.astype(BF16)
    w_gate_up_b = w_gate_up.astype(BF16)
    w_down_b = w_down.astype(BF16)
    per_layer_row = lambda a: a[:, None, :]
    lam_vecs = tuple(per_layer_row(a) for a in (lambda_q1, lambda_k1, lambda_q2, lambda_k2))
    sub_g = per_layer_row(subln_g)
    norms = tuple(per_layer_row(a) for a in (ln1_g, ln1_b, ln2_g, ln2_b))
    cache_kt = cache_k.transpose(0, 1, 3, 4, 2).reshape(depth, n_pool, attn_dim, page_size)
    cache_vi = cache_v.reshape(depth, n_pool, page_size * N_HEADS, V_DIM)

    cos_m, sin_m = _rope_tables(N_META + jnp.arange(seq, dtype=jnp.int32))
    cost_m, sint_m = cos_m[:, :HEAD_DIM].T, sin_m[:, :HEAD_DIM].T
    pos_side = jnp.concatenate([jnp.tile(past_len + jnp.arange(dec_seq, dtype=jnp.int32), dec_batch),
                                jnp.arange(N_META, dtype=jnp.int32)])
    cos_s, sin_s = _rope_tables(pos_side)

    j_in_seq = jnp.concatenate([jnp.tile(jnp.arange(dec_seq, dtype=jnp.int32), dec_batch),
                                jnp.arange(N_META, dtype=jnp.int32)])
    sel1 = (j_in_seq < 1).astype(F32)[:, None]
    sel2 = (j_in_seq < 2).astype(F32)[:, None]

    def prefixes(state):
        z = jnp.zeros((dec_batch, dec_seq, conv_dim), F32)
        p1 = z.at[:, 0].set(state[:, 1]).reshape(n_sample, conv_dim)
        p2 = z.at[:, 0].set(state[:, 0]).at[:, 1].set(state[:, 1]).reshape(n_sample, conv_dim)
        zm = jnp.zeros((N_META, conv_dim), F32)
        return jnp.concatenate([p1, zm]), jnp.concatenate([p2, zm])

    x_main = x_prompt.reshape(batch * seq, d_model)
    x_side = jnp.concatenate([x_sample.reshape(n_sample, d_model), meta_tokens.astype(F32)])

    outs = {name: [] for name in ("cp", "ks", "vs", "cs")}
    prompt_cache, kt_tails, meta_v = None, [], []
    for l in range(depth):
        lam_init = 0.8 - 0.6 * math.exp(-0.3 * l)

        pre1, pre2 = prefixes(state_conv[l])
        yc_s, q_s, k_s, v_s, u_s = _inproj_side(x_side, w_in_b, conv_w, cos_s, sin_s, pre1, pre2, sel1, sel2,
                                                layer=l)
        k_meta, v_meta, u_meta = k_s[n_sample:], v_s[n_sample:], u_s[n_sample:]

        kmt = jnp.concatenate([k_meta, jnp.zeros((LANES - N_META, attn_dim), F32)]).T
        yc_m, q_m, kt_cache, v_cache, ktb_m, vb_m, cstate, kt_tail = _inproj_main(
            x_main, w_in_b, w_kt, conv_w, cos_m, sin_m, cost_m, sint_m,
            u_meta[N_META - (CONV_W - 1):], kmt, prompt_cache, seq=seq, tm=tm, layer=l, depth=depth)
        prompt_cache = (kt_cache, v_cache)
        kt_tails.append(kt_tail[:, :, :N_META])
        meta_v.append(v_meta.reshape(N_META * N_HEADS, V_DIM))

        pad = jnp.zeros((LANES - N_META, attn_dim), BF16)
        ya_m = _prompt_attention(q_m, ktb_m, vb_m, kmt.astype(BF16),
                                 jnp.concatenate([v_meta.astype(BF16), pad]),
                                 lam_vecs, sub_g, seq=seq, tq=tq, tk=tk, layer=l, lam_init=lam_init)
        side_attend = functools.partial(_side_attention, q_s, k_s, v_s, lam_vecs, sub_g, layer=l, lam_init=lam_init)
        ya_sample = side_attend(n=dec_seq, first_row=0, rows=n_sample,
                                cache=(cache_kt, cache_vi, page_table), pages_per_step=pages_per_step)
        ya_meta = side_attend(n=N_META, first_row=n_sample, rows=N_META)
        ya_s = jnp.concatenate([ya_sample, ya_meta])

        post = functools.partial(_post_attention, wo=w_out_b, wgu=w_gate_up_b, wd=w_down_b, g1=norms[0],
                                 b1=norms[1], g2=norms[2], b2=norms[3], alpha=alpha, layer=l)
        x_main = post(yc_m, ya_m, x_main, tm=tm)
        x_side = post(yc_s, ya_s, x_side, tm=n_side)

        outs["cp"].append(cstate)
        outs["ks"].append(k_s[:n_sample].reshape(dec_batch, dec_seq, N_SUB, HEAD_DIM))
        outs["vs"].append(v_s[:n_sample].reshape(dec_batch, dec_seq, N_HEADS, V_DIM))
        outs["cs"].append(u_s[:n_sample].reshape(dec_batch, dec_seq, conv_dim)[:, dec_seq - (CONV_W - 1):])

    kt_cache, v_cache = prompt_cache
    new_k_prompt = lax.dynamic_update_slice(kt_cache, jnp.stack(kt_tails), (0, 0, 0, seq)).reshape(
        depth, batch, N_SUB, HEAD_DIM, N_META + seq).transpose(0, 1, 4, 2, 3)
    meta_v_b = jnp.broadcast_to(jnp.stack(meta_v)[:, None], (depth, batch, N_META * N_HEADS, V_DIM))
    new_v_prompt = lax.dynamic_update_slice(v_cache, meta_v_b, (0, 0, 0, 0)).reshape(
        depth, batch, N_META + seq, N_HEADS, V_DIM)

    y_prompt = x_main.reshape(batch, seq, d_model)
    y_sample = x_side[:n_sample].reshape(dec_batch, dec_seq, d_model)
    return (y_prompt, y_sample, new_k_prompt, new_v_prompt, jnp.stack(outs["cp"]),
            jnp.stack(outs["ks"]), jnp.stack(outs["vs"]), jnp.stack(outs["cs"]))
```

```python
import functools
import math

import jax
import jax.numpy as jnp
from jax import lax
from jax.experimental import pallas as pl
from jax.experimental.pallas import tpu as pltpu

F32 = jnp.float32
BF16 = jnp.bfloat16

N_META = 16
CONV_W = 3
N_HEADS = 4
N_SUB = 2 * N_HEADS
HEAD_DIM = 64
V_DIM = 2 * HEAD_DIM
ROPE_THETA = 10000.0
LN_EPS = 1e-5
NEG_INF = -1e30
LOG2_E = math.log2(math.e)
LANES = 128
FF_CHUNK = 256
ATTN_TQ = 1024
ATTN_TK = 512
ROW_TILE = 512
PAGES_PER_STEP = 32
SUBLANES = 8
U0 = SUBLANES
VMEM_LIMIT = 56 * 1024 * 1024


def _params(*sem):
    return pltpu.CompilerParams(dimension_semantics=sem, vmem_limit_bytes=VMEM_LIMIT)


def _full(shape):
    return pl.BlockSpec(shape, lambda *_: (0,) * len(shape))


def _of_layer(stacked, layer):
    tail = stacked.shape[1:]
    return pl.BlockSpec((None,) + tail, lambda *_: (layer,) + (0,) * len(tail))


def _rope128(x, cos, sin_signed):
    lane = lax.broadcasted_iota(jnp.int32, x.shape, 1)
    swapped = jnp.where((lane & (HEAD_DIM // 2)) == 0,
                        pltpu.roll(x, LANES - HEAD_DIM // 2, 1), pltpu.roll(x, HEAD_DIM // 2, 1))
    return x * cos + swapped * sin_signed


def _layernorm(h, g, b):
    mu = jnp.mean(h, axis=-1, keepdims=True)
    d = h - mu
    var = jnp.mean(d * d, axis=-1, keepdims=True)
    return d * lax.rsqrt(var + LN_EPS) * g + b


def _lam(lq1, lk1, lq2, lk2, lam_init):
    a = jnp.sum(lq1[...] * lk1[...], axis=1, keepdims=True)
    b = jnp.sum(lq2[...] * lk2[...], axis=1, keepdims=True)
    return jnp.exp(a) - jnp.exp(b) + lam_init


def _diff_norm(o1, o2, lam, g, lam_init):
    o = o1 - lam * o2
    o = o * lax.rsqrt(jnp.mean(o * o, axis=-1, keepdims=True) + LN_EPS)
    return o * g * (1.0 - lam_init)


def _lanes(x, width):
    reps = width // LANES
    return x if reps == 1 else jnp.concatenate([x] * reps, axis=1)


def _online_softmax(s, m_prev, l_prev):
    m_new = jnp.maximum(m_prev, jnp.max(s, axis=1, keepdims=True))
    alpha = jnp.exp(m_prev - m_new)
    p = jnp.exp(s - _lanes(m_new, s.shape[1]))
    l_new = alpha * l_prev + jnp.sum(p, axis=1, keepdims=True)
    return p, alpha, m_new, l_new


def _dot_nt(a, b):
    return lax.dot_general(a, b, (((1,), (1,)), ((), ())), preferred_element_type=F32)


def _project(x_ref, w_ref, conv_dim):
    xb = x_ref[...].astype(BF16)

    def proj(seg):
        return jnp.dot(xb, w_ref[:, seg * conv_dim:(seg + 1) * conv_dim], preferred_element_type=F32)

    return proj


def _rope_store(val, cos, sin, scale, refs):
    for c in range(val.shape[1] // LANES):
        sl = slice(c * LANES, (c + 1) * LANES)
        r = _rope128(val[:, sl], cos, sin)
        if scale != 1.0:
            r = r * scale
        for ref in refs:
            ref[:, sl] = r.astype(ref.dtype)


def _inproj_main_kernel(x_ref, w_ref, wkt_ref, cw_ref, cos_ref, sin_ref, cost_ref, sint_ref, upre_ref,
                        kmt_ref, *rest, tm, tiles_per_seq, conv_dim):
    yc_ref, q_ref, kt_ref, v_ref, ktb_ref, vb_ref, cs_ref, ktail_ref, ubuf = rest[-9:]

    hist = slice(U0 - (CONV_W - 1), U0)

    @pl.when(pl.program_id(0) % tiles_per_seq == 0)
    def _():
        ubuf[hist, :] = upre_ref[...]
        ktail_ref[...] = kmt_ref[...]

    proj = _project(x_ref, w_ref, conv_dim)
    u = proj(2) * proj(0)
    ubuf[U0:U0 + tm, :] = u
    cw = cw_ref[...]
    conv = ubuf[hist.start:hist.start + tm, :] * cw[0:1]
    for j in range(1, CONV_W):
        conv = conv + ubuf[hist.start + j:hist.start + j + tm, :] * cw[j:j + 1]
    yc_ref[...] = (proj(1) * conv).astype(yc_ref.dtype)
    tail = ubuf[U0 + tm - (CONV_W - 1):U0 + tm, :]
    cs_ref[...] = tail
    ubuf[hist, :] = tail

    cos = cos_ref[...]
    sin = sin_ref[...]
    _rope_store(proj(3), cos, sin, HEAD_DIM ** -0.5 * LOG2_E, (q_ref,))

    kt = _dot_nt(wkt_ref[...], x_ref[...].astype(BF16))
    cost = cost_ref[...]
    sint = sint_ref[...]
    half = HEAD_DIM // 2
    lane = lax.broadcasted_iota(jnp.int32, (HEAD_DIM, LANES), 1)
    for h in range(N_SUB):
        hd = slice(h * HEAD_DIM, (h + 1) * HEAD_DIM)
        blk = kt[hd, :]
        swapped = jnp.concatenate([blk[half:], blk[:half]], axis=0)
        r = blk * cost + swapped * sint
        ktb_ref[hd, :] = r.astype(ktb_ref.dtype)
        shifted = pltpu.roll(r, N_META, 1)
        kt_ref[hd, :LANES] = jnp.where(lane < N_META, ktail_ref[hd, :], shifted[:, :LANES])
        kt_ref[hd, LANES:] = shifted[:, LANES:]
        ktail_ref[hd, :] = shifted[:, :LANES]

    vv = proj(5)
    vb_ref[...] = vv.astype(vb_ref.dtype)
    for hv in range(N_HEADS):
        v_ref[0, 0, pl.ds(hv, tm, stride=N_HEADS), :] = vv[:, hv * V_DIM:(hv + 1) * V_DIM]


def _inproj_side_kernel(x_ref, w_ref, cw_ref, cos_ref, sin_ref, pre1_ref, pre2_ref, sel1_ref, sel2_ref,
                        yc_ref, q_ref, k_ref, v_ref, u_ref, ubuf, *, tm, conv_dim):
    proj = _project(x_ref, w_ref, conv_dim)
    u = proj(2) * proj(0)
    u_ref[...] = u
    ubuf[0:U0, :] = jnp.zeros((U0, conv_dim), F32)
    ubuf[U0:U0 + tm, :] = u
    um2 = jnp.where(sel2_ref[...] > 0.5, pre2_ref[...], ubuf[U0 - 2:U0 - 2 + tm, :])
    um1 = jnp.where(sel1_ref[...] > 0.5, pre1_ref[...], ubuf[U0 - 1:U0 - 1 + tm, :])
    cw = cw_ref[...]
    conv = um2 * cw[0:1] + um1 * cw[1:2]
    conv = conv + u * cw[2:3]
    yc_ref[...] = proj(1) * conv
    cos = cos_ref[...]
    sin = sin_ref[...]
    _rope_store(proj(3), cos, sin, HEAD_DIM ** -0.5, (q_ref,))
    _rope_store(proj(4), cos, sin, 1.0, (k_ref,))
    v_ref[...] = proj(5)


def _inproj_main(x, w, wkt, cw, cos, sin, cost, sint, upre, kmt, cache_out, *, seq, tm, layer, depth):
    rows, d_model = x.shape
    conv_dim = cw.shape[2]
    attn_dim = wkt.shape[1]
    tiles_per_seq = seq // tm
    n_seq = rows // seq
    seq_of = lambda i: i // tiles_per_seq
    tile_of = lambda i: i % tiles_per_seq
    row_blk = lambda width: pl.BlockSpec((tm, width), lambda i: (i, 0))
    tab_blk = pl.BlockSpec((tm, LANES), lambda i: (tile_of(i), 0))
    tabt_blk = pl.BlockSpec((HEAD_DIM, tm), lambda i: (0, tile_of(i)))
    kt_blk = pl.BlockSpec((None, attn_dim, tm), lambda i: (seq_of(i), 0, tile_of(i)))
    kt_out_blk = pl.BlockSpec((None, None, attn_dim, tm), lambda i: (layer, seq_of(i), 0, tile_of(i)))
    v_out_blk = pl.BlockSpec((pl.Element(1), pl.Element(1), pl.Element(tm * N_HEADS), pl.Element(V_DIM)),
                             lambda i: (layer, seq_of(i),
                                        pl.multiple_of((N_META + tile_of(i) * tm) * N_HEADS, 8), 0))
    out_shape = (
        jax.ShapeDtypeStruct((rows, conv_dim), BF16),
        jax.ShapeDtypeStruct((rows, attn_dim), BF16),
        jax.ShapeDtypeStruct((depth, n_seq, attn_dim, N_META + seq), F32),
        jax.ShapeDtypeStruct((depth, n_seq, (N_META + seq) * N_HEADS, V_DIM), F32),
        jax.ShapeDtypeStruct((n_seq, attn_dim, seq), BF16),
        jax.ShapeDtypeStruct((rows, attn_dim), BF16),
        jax.ShapeDtypeStruct((n_seq, CONV_W - 1, conv_dim), F32),
        jax.ShapeDtypeStruct((n_seq, attn_dim, LANES), F32),
    )
    args = [x, w, wkt, cw, cos, sin, cost, sint, upre, kmt]
    in_specs = [row_blk(d_model), _of_layer(w, layer), _of_layer(wkt, layer), _of_layer(cw, layer),
                tab_blk, tab_blk, tabt_blk, tabt_blk, _full(upre.shape), _full(kmt.shape)]
    aliases = {}
    if cache_out is not None:
        aliases = {len(args): 2, len(args) + 1: 3}
        args += list(cache_out)
        in_specs += [pl.BlockSpec(memory_space=pl.ANY)] * 2
    return pl.pallas_call(
        functools.partial(_inproj_main_kernel, tm=tm, tiles_per_seq=tiles_per_seq, conv_dim=conv_dim),
        grid=(rows // tm,),
        in_specs=in_specs,
        out_specs=(row_blk(conv_dim), row_blk(attn_dim), kt_out_blk, v_out_blk, kt_blk, row_blk(attn_dim),
                   pl.BlockSpec((None, CONV_W - 1, conv_dim), lambda i: (seq_of(i), 0, 0)),
                   pl.BlockSpec((None, attn_dim, LANES), lambda i: (seq_of(i), 0, 0))),
        out_shape=out_shape,
        input_output_aliases=aliases,
        scratch_shapes=[pltpu.VMEM((U0 + tm, conv_dim), F32)],
        compiler_params=_params("arbitrary"),
        name="inproj_main",
    )(*args)


def _inproj_side(x, w, cw, cos, sin, pre1, pre2, sel1, sel2, *, layer):
    rows, _ = x.shape
    conv_dim = cw.shape[2]
    args = (x, w, cw, cos, sin, pre1, pre2, sel1, sel2)
    out = jax.ShapeDtypeStruct((rows, conv_dim), F32)
    return pl.pallas_call(
        functools.partial(_inproj_side_kernel, tm=rows, conv_dim=conv_dim),
        grid=(1,),
        in_specs=[_of_layer(a, layer) if a is w or a is cw else _full(a.shape) for a in args],
        out_specs=tuple(_full(out.shape) for _ in range(5)),
        out_shape=(out,) * 5,
        scratch_shapes=[pltpu.VMEM((U0 + rows, conv_dim), F32)],
        compiler_params=_params("arbitrary"),
        name="inproj_side",
    )(*args)


def _prompt_attn_kernel(qi_ref, kj_ref, q_ref, kt_ref, v_ref, kmt_ref, vm_ref,
                        lq1, lk1, lq2, lk2, sg_ref, o_ref, m_sc, acc_sc, *, tq, tk, lam_init):
    step = pl.program_id(1)
    i = qi_ref[step]
    j = kj_ref[step]

    def head_slices(h):
        return slice(h * HEAD_DIM, (h + 1) * HEAD_DIM), slice((h // 2) * V_DIM, (h // 2 + 1) * V_DIM)

    def pv(p, v_blk):
        ones = jnp.ones((v_blk.shape[0], LANES), BF16)
        return jnp.dot(p.astype(BF16), jnp.concatenate([v_blk, ones], axis=1), preferred_element_type=F32)

    @pl.when(j == 0)
    def _():
        col = lax.broadcasted_iota(jnp.int32, (tq, LANES), 1)
        for h in range(N_SUB):
            qk, vv = head_slices(h)
            s = jnp.dot(q_ref[:, qk], kmt_ref[qk, :], preferred_element_type=F32)
            s = jnp.where(col < N_META, s, NEG_INF)
            m = jnp.max(s, axis=1, keepdims=True)
            m_sc[h] = jnp.broadcast_to(m, (tq, LANES))
            acc_sc[h] = pv(jnp.exp2(s - m), vm_ref[:, vv])

    def process(rows, triangular):
        n_rows = rows.stop - rows.start
        if triangular:
            row = lax.broadcasted_iota(jnp.int32, (n_rows, tk), 0)
            col = lax.broadcasted_iota(jnp.int32, (n_rows, tk), 1)
            visible = col <= row
        for h in range(N_SUB):
            qk, vv = head_slices(h)
            s = jnp.dot(q_ref[rows, qk], kt_ref[qk, :], preferred_element_type=F32)
            if triangular:
                s = jnp.where(visible, s, NEG_INF)
            m_prev = m_sc[h, rows]
            m_new = jnp.maximum(m_prev, jnp.max(s, axis=1, keepdims=True))
            alpha = jnp.exp2(m_prev - m_new)
            p = jnp.exp2(s - _lanes(m_new, tk))
            m_sc[h, rows] = m_new
            acc_sc[h, rows] = acc_sc[h, rows] * _lanes(alpha, V_DIM + LANES) + pv(p, v_ref[:, vv])

    per_q = tq // tk
    j_local = j - i * per_q

    @pl.when(j_local < 0)
    def _():
        process(slice(0, tq), False)

    for jl in range(per_q):
        @pl.when(j_local == jl)
        def _(jl=jl):
            process(slice(jl * tk, (jl + 1) * tk), True)
            if jl + 1 < per_q:
                process(slice((jl + 1) * tk, tq), False)

    @pl.when(j_local == per_q - 1)
    def _():
        lam = _lam(lq1, lk1, lq2, lk2, lam_init)
        g = sg_ref[...]
        for hv in range(N_HEADS):
            a1 = acc_sc[2 * hv]
            a2 = acc_sc[2 * hv + 1]
            o1 = a1[:, :V_DIM] / a1[:, V_DIM:]
            o2 = a2[:, :V_DIM] / a2[:, V_DIM:]
            o_ref[:, hv * V_DIM:(hv + 1) * V_DIM] = _diff_norm(o1, o2, lam, g, lam_init).astype(o_ref.dtype)


def _prompt_attention(q, kt, v, kt_meta, v_meta, lam_vecs, sub_g, *, seq, tq, tk, layer, lam_init):
    rows, attn_dim = q.shape
    n_seq = rows // seq
    nq, nk = seq // tq, seq // tk
    per_q = tq // tk
    pairs = [(i, j) for i in range(nq) for j in range((i + 1) * per_q)]
    qi = jnp.asarray([p[0] for p in pairs], jnp.int32)
    kj = jnp.asarray([p[1] for p in pairs], jnp.int32)
    q_blk = pl.BlockSpec((tq, attn_dim), lambda b, s, qi, kj: (b * nq + qi[s], 0))
    v_blk = pl.BlockSpec((tk, attn_dim), lambda b, s, qi, kj: (b * nk + kj[s], 0))
    kt_blk = pl.BlockSpec((None, attn_dim, tk), lambda b, s, qi, kj: (b, 0, kj[s]))
    small = [_full(kt_meta.shape), _full(v_meta.shape)] + [_of_layer(a, layer) for a in (*lam_vecs, sub_g)]
    grid_spec = pltpu.PrefetchScalarGridSpec(
        num_scalar_prefetch=2,
        grid=(n_seq, len(pairs)),
        in_specs=[q_blk, kt_blk, v_blk] + small,
        out_specs=q_blk,
        scratch_shapes=[pltpu.VMEM((N_SUB, tq, LANES), F32), pltpu.VMEM((N_SUB, tq, V_DIM + LANES), F32)],
    )
    return pl.pallas_call(
        functools.partial(_prompt_attn_kernel, tq=tq, tk=tk, lam_init=lam_init),
        grid_spec=grid_spec,
        out_shape=jax.ShapeDtypeStruct((rows, attn_dim), BF16),
        compiler_params=_params("parallel", "arbitrary"),
        name="prompt_attn",
    )(qi, kj, q, kt, v, kt_meta, v_meta, *lam_vecs, sub_g)


def _side_attn_kernel(pt_ref, q_ref, kn_ref, vn_ref, *rest, n, pages_per_step, lam_init):
    del pt_ref
    k_refs = rest[:pages_per_step]
    v_refs = rest[pages_per_step:2 * pages_per_step]
    lq1, lk1, lq2, lk2, sg_ref, o_ref, qbd_sc, m_sc, l_sc, acc_sc = rest[2 * pages_per_step:]
    g = pl.program_id(1)
    rows = N_SUB * n
    attn_dim = q_ref.shape[1]

    @pl.when(g == 0)
    def _():
        qt = jnp.concatenate([q_ref[...]] * N_SUB, axis=0)
        row = lax.broadcasted_iota(jnp.int32, (rows, attn_dim), 0)
        lane = lax.broadcasted_iota(jnp.int32, (rows, attn_dim), 1)
        own = (row // n) == (lane // HEAD_DIM)
        qbd_sc[...] = jnp.where(own, qt, 0.0).astype(qbd_sc.dtype)
        m_sc[...] = jnp.full(m_sc.shape, NEG_INF, F32)
        l_sc[...] = jnp.zeros(l_sc.shape, F32)
        acc_sc[...] = jnp.zeros(acc_sc.shape, F32)

    def update(s, value_block):
        p, alpha, m_new, l_new = _online_softmax(s, m_sc[...], l_sc[...])
        m_sc[...] = m_new
        l_sc[...] = l_new
        for hv in range(N_HEADS):
            pair = slice(2 * hv * n, 2 * (hv + 1) * n)
            acc = acc_sc[pair, :] * alpha[pair]
            for c in range(s.shape[1] // LANES):
                acc = acc + jnp.dot(p[pair, c * LANES:(c + 1) * LANES].astype(BF16),
                                    value_block(hv, c).astype(BF16), preferred_element_type=F32)
            acc_sc[pair, :] = acc

    if pages_per_step:
        qbd = qbd_sc[...]
        s = jnp.concatenate([jnp.dot(qbd, k_ref[...].astype(BF16), preferred_element_type=F32)
                             for k_ref in k_refs], axis=1)
        update(s, lambda hv, c: v_refs[c][pl.ds(hv, LANES, stride=N_HEADS), :])

    @pl.when(g == pl.num_programs(1) - 1)
    def _():
        pad = jnp.zeros((LANES - n, attn_dim), F32)
        kn = jnp.concatenate([kn_ref[...], pad], axis=0).astype(BF16)
        vn = jnp.concatenate([vn_ref[...], pad], axis=0)
        row = lax.broadcasted_iota(jnp.int32, (rows, LANES), 0)
        col = lax.broadcasted_iota(jnp.int32, (rows, LANES), 1)
        s = jnp.where(col <= (row % n), _dot_nt(qbd_sc[...], kn), NEG_INF)
        update(s, lambda hv, c: vn[:, hv * V_DIM:(hv + 1) * V_DIM])
        lam = _lam(lq1, lk1, lq2, lk2, lam_init)
        gain = sg_ref[...]
        out = acc_sc[...] / l_sc[...]
        for hv in range(N_HEADS):
            r0 = 2 * hv * n
            o_ref[:, hv * V_DIM:(hv + 1) * V_DIM] = _diff_norm(out[r0:r0 + n], out[r0 + n:r0 + 2 * n],
                                                               lam, gain, lam_init)


def _side_attention(q, k_new, v_new, lam_vecs, sub_g, *, n, first_row, rows, layer, lam_init,
                    cache=None, pages_per_step=0):
    attn_dim = q.shape[1]
    n_seq = rows // n
    first_blk = first_row // n
    tok_blk = pl.BlockSpec((n, attn_dim), lambda b, g, pt: (first_blk + b, 0))
    out_blk = pl.BlockSpec((n, attn_dim), lambda b, g, pt: (b, 0))
    small = [_of_layer(a, layer) for a in (*lam_vecs, sub_g)]
    if cache is None:
        page_table = jnp.zeros((1,), jnp.int32)
        steps, kv_specs, kv_args = 1, [], []
    else:
        cache_k, cache_v, page_table = cache
        n_pages = page_table.shape[1]
        steps = n_pages // pages_per_step
        page_table = page_table.reshape(-1)

        def page_blk(jj):
            return pl.BlockSpec(
                (None, None) + cache_k.shape[2:],
                lambda b, g, pt: (layer, pt[b * n_pages + g * pages_per_step + jj], 0, 0))

        kv_specs = [page_blk(jj) for jj in range(pages_per_step)] * 2
        kv_args = [cache_k] * pages_per_step + [cache_v] * pages_per_step
    grid_spec = pltpu.PrefetchScalarGridSpec(
        num_scalar_prefetch=1,
        grid=(n_seq, steps),
        in_specs=[tok_blk, tok_blk, tok_blk] + kv_specs + small,
        out_specs=out_blk,
        scratch_shapes=[pltpu.VMEM((N_SUB * n, attn_dim), BF16), pltpu.VMEM((N_SUB * n, LANES), F32),
                        pltpu.VMEM((N_SUB * n, LANES), F32), pltpu.VMEM((N_SUB * n, V_DIM), F32)],
    )
    return pl.pallas_call(
        functools.partial(_side_attn_kernel, n=n, pages_per_step=pages_per_step if cache else 0,
                          lam_init=lam_init),
        grid_spec=grid_spec,
        out_shape=jax.ShapeDtypeStruct((rows, attn_dim), F32),
        compiler_params=_params("parallel", "arbitrary"),
        name="sample_attn" if cache else "meta_attn",
    )(page_table, q, k_new, v_new, *kv_args, *lam_vecs, sub_g)


def _post_kernel(yc_ref, ya_ref, x_ref, wo_ref, wgu_ref, wd_ref, g1, b1, g2, b2,
                 o_ref, x1_sc, x1b_sc, *, alpha, conv_dim):
    d_ff = wd_ref.shape[0]
    mix = jnp.dot(yc_ref[...].astype(BF16), wo_ref[0:conv_dim, :], preferred_element_type=F32)
    mix = mix + jnp.dot(ya_ref[...].astype(BF16), wo_ref[conv_dim:, :], preferred_element_type=F32)
    x1 = _layernorm(alpha * x_ref[...] + mix, g1[...], b1[...])
    x1_sc[...] = x1
    x1b_sc[...] = x1.astype(BF16)

    ffn = None
    for c0 in range(0, d_ff, FF_CHUNK):
        x1b = x1b_sc[...]
        gate = jnp.dot(x1b, wgu_ref[:, c0:c0 + FF_CHUNK], preferred_element_type=F32)
        up = jnp.dot(x1b, wgu_ref[:, d_ff + c0:d_ff + c0 + FF_CHUNK], preferred_element_type=F32)
        hidden = (gate / (1.0 + jnp.exp(-gate))) * up
        part = jnp.dot(hidden.astype(BF16), wd_ref[c0:c0 + FF_CHUNK, :], preferred_element_type=F32)
        ffn = part if ffn is None else ffn + part
    o_ref[...] = _layernorm(alpha * x1_sc[...] + ffn, g2[...], b2[...])


def _post_attention(yc, ya, x, wo, wgu, wd, g1, b1, g2, b2, *, tm, alpha, layer):
    rows, d_model = x.shape
    conv_dim = yc.shape[1]
    assert wd.shape[1] % FF_CHUNK == 0
    row_blk = lambda width: pl.BlockSpec((tm, width), lambda i: (i, 0))
    consts = (wo, wgu, wd, g1, b1, g2, b2)
    return pl.pallas_call(
        functools.partial(_post_kernel, alpha=alpha, conv_dim=conv_dim),
        grid=(rows // tm,),
        in_specs=[row_blk(conv_dim), row_blk(ya.shape[1]), row_blk(d_model)] + [_of_layer(a, layer) for a in consts],
        out_specs=row_blk(d_model),
        out_shape=jax.ShapeDtypeStruct((rows, d_model), F32),
        scratch_shapes=[pltpu.VMEM((tm, d_model), F32), pltpu.VMEM((tm, d_model), BF16)],
        compiler_params=_params("parallel"),
        name="post_main" if rows > tm else "post_side",
    )(yc, ya, x, *consts)


def _rope_tables(pos):
    inv = 1.0 / (ROPE_THETA ** (jnp.arange(0, HEAD_DIM, 2, dtype=F32) / HEAD_DIM))
    ang = pos.astype(F32)[:, None] * inv[None, :]
    cos = jnp.cos(ang)
    sin = jnp.sin(ang)
    return jnp.concatenate([cos] * 4, axis=-1), jnp.concatenate([-sin, sin] * 2, axis=-1)


def _row_tile(n, target):
    t = min(n, target)
    while n % t:
        t //= 2
    return t


def kernel(x_prompt, x_sample, cache_k, cache_v, state_conv, page_table, meta_tokens, w_in, conv_w, w_out, lambda_q1, lambda_k1, lambda_q2, lambda_k2, subln_g, ln1_g, ln1_b, ln2_g, ln2_b, w_gate_up, w_down):
    batch, seq, d_model = x_prompt.shape
    dec_batch, dec_seq, _ = x_sample.shape
    depth = w_in.shape[0]
    conv_dim = conv_w.shape[2]
    attn_dim = N_SUB * HEAD_DIM
    d_ff = w_down.shape[1]
    n_pool, page_size = cache_k.shape[1], cache_k.shape[2]
    past_len = page_table.shape[1] * page_size
    alpha = (2 * depth) ** 0.25
    n_sample = dec_batch * dec_seq
    n_side = n_sample + N_META
    tm = _row_tile(seq, ROW_TILE)
    tq = _row_tile(seq, ATTN_TQ)
    tk = _row_tile(tq, ATTN_TK)
    pages_per_step = _row_tile(page_table.shape[1], PAGES_PER_STEP)
    assert page_size == LANES and attn_dim == conv_dim
    assert n_sample % N_META == 0

    w_in_b = w_in.astype(BF16)
    k_col = 3 * conv_dim + attn_dim
    w_kt = w_in_b[:, :, k_col:k_col + attn_dim].transpose(0, 2, 1)
    w_out_b = w_out.astype(BF16)
    w_gate_up_b = w_gate_up.astype(BF16)
    w_down_b = w_down.astype(BF16)
    per_layer_row = lambda a: a[:, None, :]
    lam_vecs = tuple(per_layer_row(a) for a in (lambda_q1, lambda_k1, lambda_q2, lambda_k2))
    sub_g = per_layer_row(subln_g)
    norms = tuple(per_layer_row(a) for a in (ln1_g, ln1_b, ln2_g, ln2_b))
    cache_kt = cache_k.transpose(0, 1, 3, 4, 2).reshape(depth, n_pool, attn_dim, page_size)
    cache_vi = cache_v.reshape(depth, n_pool, page_size * N_HEADS, V_DIM)

    cos_m, sin_m = _rope_tables(N_META + jnp.arange(seq, dtype=jnp.int32))
    cost_m, sint_m = cos_m[:, :HEAD_DIM].T, sin_m[:, :HEAD_DIM].T
    pos_side = jnp.concatenate([jnp.tile(past_len + jnp.arange(dec_seq, dtype=jnp.int32), dec_batch),
                                jnp.arange(N_META, dtype=jnp.int32)])
    cos_s, sin_s = _rope_tables(pos_side)

    j_in_seq = jnp.concatenate([jnp.tile(jnp.arange(dec_seq, dtype=jnp.int32), dec_batch),
                                jnp.arange(N_META, dtype=jnp.int32)])
    sel1 = (j_in_seq < 1).astype(F32)[:, None]
    sel2 = (j_in_seq < 2).astype(F32)[:, None]

    def prefixes(state):
        z = jnp.zeros((dec_batch, dec_seq, conv_dim), F32)
        p1 = z.at[:, 0].set(state[:, 1]).reshape(n_sample, conv_dim)
        p2 = z.at[:, 0].set(state[:, 0]).at[:, 1].set(state[:, 1]).reshape(n_sample, conv_dim)
        zm = jnp.zeros((N_META, conv_dim), F32)
        return jnp.concatenate([p1, zm]), jnp.concatenate([p2, zm])

    x_main = x_prompt.reshape(batch * seq, d_model)
    x_side = jnp.concatenate([x_sample.reshape(n_sample, d_model), meta_tokens.astype(F32)])

    outs = {name: [] for name in ("cp", "ks", "vs", "cs")}
    prompt_cache, kt_tails, meta_v = None, [], []
    for l in range(depth):
        lam_init = 0.8 - 0.6 * math.exp(-0.3 * l)

        pre1, pre2 = prefixes(state_conv[l])
        yc_s, q_s, k_s, v_s, u_s = _inproj_side(x_side, w_in_b, conv_w, cos_s, sin_s, pre1, pre2, sel1, sel2,
                                                layer=l)
        k_meta, v_meta, u_meta = k_s[n_sample:], v_s[n_sample:], u_s[n_sample:]

        kmt = jnp.concatenate([k_meta, jnp.zeros((LANES - N_META, attn_dim), F32)]).T
        yc_m, q_m, kt_cache, v_cache, ktb_m, vb_m, cstate, kt_tail = _inproj_main(
            x_main, w_in_b, w_kt, conv_w, cos_m, sin_m, cost_m, sint_m,
            u_meta[N_META - (CONV_W - 1):], kmt, prompt_cache, seq=seq, tm=tm, layer=l, depth=depth)
        prompt_cache = (kt_cache, v_cache)
        kt_tails.append(kt_tail[:, :, :N_META])
        meta_v.append(v_meta.reshape(N_META * N_HEADS, V_DIM))

        pad = jnp.zeros((LANES - N_META, attn_dim), BF16)
        ya_m = _prompt_attention(q_m, ktb_m, vb_m, kmt.astype(BF16),
                                 jnp.concatenate([v_meta.astype(BF16), pad]),
                                 lam_vecs, sub_g, seq=seq, tq=tq, tk=tk, layer=l, lam_init=lam_init)
        side_attend = functools.partial(_side_attention, q_s, k_s, v_s, lam_vecs, sub_g, layer=l, lam_init=lam_init)
        ya_sample = side_attend(n=dec_seq, first_row=0, rows=n_sample,
                                cache=(cache_kt, cache_vi, page_table), pages_per_step=pages_per_step)
        ya_meta = side_attend(n=N_META, first_row=n_sample, rows=N_META)
        ya_s = jnp.concatenate([ya_sample, ya_meta])

        post = functools.partial(_post_attention, wo=w_out_b, wgu=w_gate_up_b, wd=w_down_b, g1=norms[0],
                                 b1=norms[1], g2=norms[2], b2=norms[3], alpha=alpha, layer=l)
        x_main = post(yc_m, ya_m, x_main, tm=tm)
        x_side = post(yc_s, ya_s, x_side, tm=n_side)

        outs["cp"].append(cstate)
        outs["ks"].append(k_s[:n_sample].reshape(dec_batch, dec_seq, N_SUB, HEAD_DIM))
        outs["vs"].append(v_s[:n_sample].reshape(dec_batch, dec_seq, N_HEADS, V_DIM))
        outs["cs"].append(u_s[:n_sample].reshape(dec_batch, dec_seq, conv_dim)[:, dec_seq - (CONV_W - 1):])

    kt_cache, v_cache = prompt_cache
    new_k_prompt = lax.dynamic_update_slice(kt_cache, jnp.stack(kt_tails), (0, 0, 0, seq)).reshape(
        depth, batch, N_SUB, HEAD_DIM, N_META + seq).transpose(0, 1, 4, 2, 3)
    meta_v_b = jnp.broadcast_to(jnp.stack(meta_v)[:, None], (depth, batch, N_META * N_HEADS, V_DIM))
    new_v_prompt = lax.dynamic_update_slice(v_cache, meta_v_b, (0, 0, 0, 0)).reshape(
        depth, batch, N_META + seq, N_HEADS, V_DIM)

    y_prompt = x_main.reshape(batch, seq, d_model)
    y_sample = x_side[:n_sample].reshape(dec_batch, dec_seq, d_model)
    return (y_prompt, y_sample, new_k_prompt, new_v_prompt, jnp.stack(outs["cp"]),
            jnp.stack(outs["ks"]), jnp.stack(outs["vs"]), jnp.stack(outs["cs"]))
```

```python
import functools
import math

import jax
import jax.numpy as jnp
from jax import lax
from jax.experimental import pallas as pl
from jax.experimental.pallas import tpu as pltpu

F32 = jnp.float32
BF16 = jnp.bfloat16

N_META = 16
CONV_W = 3
N_HEADS = 4
N_SUB = 2 * N_HEADS
HEAD_DIM = 64
V_DIM = 2 * HEAD_DIM
ROPE_THETA = 10000.0
LN_EPS = 1e-5
NEG_INF = -1e30
LOG2_E = math.log2(math.e)
LANES = 128
FF_CHUNK = 256
ATTN_TQ = 1024
ATTN_TK = 512
ROW_TILE = 512
PAGES_PER_STEP = 32
SUBLANES = 8
U0 = SUBLANES
VMEM_LIMIT = 56 * 1024 * 1024


def _params(*sem):
    return pltpu.CompilerParams(dimension_semantics=sem, vmem_limit_bytes=VMEM_LIMIT)


def _full(shape):
    return pl.BlockSpec(shape, lambda *_: (0,) * len(shape))


def _of_layer(stacked, layer):
    tail = stacked.shape[1:]
    return pl.BlockSpec((None,) + tail, lambda *_: (layer,) + (0,) * len(tail))


def _rope128(x, cos, sin_signed):
    lane = lax.broadcasted_iota(jnp.int32, x.shape, 1)
    swapped = jnp.where((lane & (HEAD_DIM // 2)) == 0,
                        pltpu.roll(x, LANES - HEAD_DIM // 2, 1), pltpu.roll(x, HEAD_DIM // 2, 1))
    return x * cos + swapped * sin_signed


def _layernorm(h, g, b):
    mu = jnp.mean(h, axis=-1, keepdims=True)
    d = h - mu
    var = jnp.mean(d * d, axis=-1, keepdims=True)
    return d * lax.rsqrt(var + LN_EPS) * g + b


def _lam(lq1, lk1, lq2, lk2, lam_init):
    a = jnp.sum(lq1[...] * lk1[...], axis=1, keepdims=True)
    b = jnp.sum(lq2[...] * lk2[...], axis=1, keepdims=True)
    return jnp.exp(a) - jnp.exp(b) + lam_init


def _diff_norm(o1, o2, lam, g, lam_init):
    o = o1 - lam * o2
    o = o * lax.rsqrt(jnp.mean(o * o, axis=-1, keepdims=True) + LN_EPS)
    return o * g * (1.0 - lam_init)


def _lanes(x, width):
    reps = width // LANES
    return x if reps == 1 else jnp.concatenate([x] * reps, axis=1)


def _online_softmax(s, m_prev, l_prev):
    m_new = jnp.maximum(m_prev, jnp.max(s, axis=1, keepdims=True))
    alpha = jnp.exp(m_prev - m_new)
    p = jnp.exp(s - _lanes(m_new, s.shape[1]))
    l_new = alpha * l_prev + jnp.sum(p, axis=1, keepdims=True)
    return p, alpha, m_new, l_new


def _dot_nt(a, b):
    return lax.dot_general(a, b, (((1,), (1,)), ((), ())), preferred_element_type=F32)


def _project(x_ref, w_ref, conv_dim):
    xb = x_ref[...].astype(BF16)

    def proj(seg):
        return jnp.dot(xb, w_ref[:, seg * conv_dim:(seg + 1) * conv_dim], preferred_element_type=F32)

    return proj


def _rope_store(val, cos, sin, scale, refs):
    for c in range(val.shape[1] // LANES):
        sl = slice(c * LANES, (c + 1) * LANES)
        r = _rope128(val[:, sl], cos, sin)
        if scale != 1.0:
            r = r * scale
        for ref in refs:
            ref[:, sl] = r.astype(ref.dtype)


def _inproj_main_kernel(x_ref, w_ref, wkt_ref, cw_ref, cos_ref, sin_ref, cost_ref, sint_ref, upre_ref,
                        kmt_ref, *rest, tm, tiles_per_seq, conv_dim):
    yc_ref, q_ref, kt_ref, v_ref, ktb_ref, vb_ref, cs_ref, ktail_ref, ubuf = rest[-9:]

    hist = slice(U0 - (CONV_W - 1), U0)

    @pl.when(pl.program_id(0) % tiles_per_seq == 0)
    def _():
        ubuf[hist, :] = upre_ref[...]
        ktail_ref[...] = kmt_ref[...]

    proj = _project(x_ref, w_ref, conv_dim)
    u = proj(2) * proj(0)
    ubuf[U0:U0 + tm, :] = u
    cw = cw_ref[...]
    conv = ubuf[hist.start:hist.start + tm, :] * cw[0:1]
    for j in range(1, CONV_W):
        conv = conv + ubuf[hist.start + j:hist.start + j + tm, :] * cw[j:j + 1]
    yc_ref[...] = (proj(1) * conv).astype(yc_ref.dtype)
    tail = ubuf[U0 + tm - (CONV_W - 1):U0 + tm, :]
    cs_ref[...] = tail
    ubuf[hist, :] = tail

    cos = cos_ref[...]
    sin = sin_ref[...]
    _rope_store(proj(3), cos, sin, HEAD_DIM ** -0.5 * LOG2_E, (q_ref,))

    kt = _dot_nt(wkt_ref[...], x_ref[...].astype(BF16))
    cost = cost_ref[...]
    sint = sint_ref[...]
    half = HEAD_DIM // 2
    lane = lax.broadcasted_iota(jnp.int32, (HEAD_DIM, LANES), 1)
    for h in range(N_SUB):
        hd = slice(h * HEAD_DIM, (h + 1) * HEAD_DIM)
        blk = kt[hd, :]
        swapped = jnp.concatenate([blk[half:], blk[:half]], axis=0)
        r = blk * cost + swapped * sint
        ktb_ref[hd, :] = r.astype(ktb_ref.dtype)
        shifted = pltpu.roll(r, N_META, 1)
        kt_ref[hd, :LANES] = jnp.where(lane < N_META, ktail_ref[hd, :], shifted[:, :LANES])
        kt_ref[hd, LANES:] = shifted[:, LANES:]
        ktail_ref[hd, :] = shifted[:, :LANES]

    vv = proj(5)
    vb_ref[...] = vv.astype(vb_ref.dtype)
    for hv in range(N_HEADS):
        v_ref[0, 0, pl.ds(hv, tm, stride=N_HEADS), :] = vv[:, hv * V_DIM:(hv + 1) * V_DIM]


def _inproj_side_kernel(x_ref, w_ref, cw_ref, cos_ref, sin_ref, pre1_ref, pre2_ref, sel1_ref, sel2_ref,
                        yc_ref, q_ref, k_ref, v_ref, u_ref, ubuf, *, tm, conv_dim):
    proj = _project(x_ref, w_ref, conv_dim)
    u = proj(2) * proj(0)
    u_ref[...] = u
    ubuf[0:U0, :] = jnp.zeros((U0, conv_dim), F32)
    ubuf[U0:U0 + tm, :] = u
    um2 = jnp.where(sel2_ref[...] > 0.5, pre2_ref[...], ubuf[U0 - 2:U0 - 2 + tm, :])
    um1 = jnp.where(sel1_ref[...] > 0.5, pre1_ref[...], ubuf[U0 - 1:U0 - 1 + tm, :])
    cw = cw_ref[...]
    conv = um2 * cw[0:1] + um1 * cw[1:2]
    conv = conv + u * cw[2:3]
    yc_ref[...] = proj(1) * conv
    cos = cos_ref[...]
    sin = sin_ref[...]
    _rope_store(proj(3), cos, sin, HEAD_DIM ** -0.5, (q_ref,))
    _rope_store(proj(4), cos, sin, 1.0, (k_ref,))
    v_ref[...] = proj(5)


def _inproj_main(x, w, wkt, cw, cos, sin, cost, sint, upre, kmt, cache_out, *, seq, tm, layer, depth):
    rows, d_model = x.shape
    conv_dim = cw.shape[2]
    attn_dim = wkt.shape[1]
    tiles_per_seq = seq // tm
    n_seq = rows // seq
    seq_of = lambda i: i // tiles_per_seq
    tile_of = lambda i: i % tiles_per_seq
    row_blk = lambda width: pl.BlockSpec((tm, width), lambda i: (i, 0))
    tab_blk = pl.BlockSpec((tm, LANES), lambda i: (tile_of(i), 0))
    tabt_blk = pl.BlockSpec((HEAD_DIM, tm), lambda i: (0, tile_of(i)))
    kt_blk = pl.BlockSpec((None, None, attn_dim, tm), lambda i: (seq_of(i), tile_of(i), 0, 0))
    kt_out_blk = pl.BlockSpec((None, None, attn_dim, tm), lambda i: (layer, seq_of(i), 0, tile_of(i)))
    v_out_blk = pl.BlockSpec((pl.Element(1), pl.Element(1), pl.Element(tm * N_HEADS), pl.Element(V_DIM)),
                             lambda i: (layer, seq_of(i),
                                        pl.multiple_of((N_META + tile_of(i) * tm) * N_HEADS, 8), 0))
    out_shape = (
        jax.ShapeDtypeStruct((rows, conv_dim), BF16),
        jax.ShapeDtypeStruct((rows, attn_dim), BF16),
        jax.ShapeDtypeStruct((depth, n_seq, attn_dim, N_META + seq), F32),
        jax.ShapeDtypeStruct((depth, n_seq, (N_META + seq) * N_HEADS, V_DIM), F32),
        jax.ShapeDtypeStruct((n_seq, tiles_per_seq, attn_dim, tm), BF16),
        jax.ShapeDtypeStruct((rows, attn_dim), BF16),
        jax.ShapeDtypeStruct((n_seq, CONV_W - 1, conv_dim), F32),
        jax.ShapeDtypeStruct((n_seq, attn_dim, LANES), F32),
    )
    args = [x, w, wkt, cw, cos, sin, cost, sint, upre, kmt]
    in_specs = [row_blk(d_model), _of_layer(w, layer), _of_layer(wkt, layer), _of_layer(cw, layer),
                tab_blk, tab_blk, tabt_blk, tabt_blk, _full(upre.shape), _full(kmt.shape)]
    aliases = {}
    if cache_out is not None:
        aliases = {len(args): 2, len(args) + 1: 3}
        args += list(cache_out)
        in_specs += [pl.BlockSpec(memory_space=pl.ANY)] * 2
    return pl.pallas_call(
        functools.partial(_inproj_main_kernel, tm=tm, tiles_per_seq=tiles_per_seq, conv_dim=conv_dim),
        grid=(rows // tm,),
        in_specs=in_specs,
        out_specs=(row_blk(conv_dim), row_blk(attn_dim), kt_out_blk, v_out_blk, kt_blk, row_blk(attn_dim),
                   pl.BlockSpec((None, CONV_W - 1, conv_dim), lambda i: (seq_of(i), 0, 0)),
                   pl.BlockSpec((None, attn_dim, LANES), lambda i: (seq_of(i), 0, 0))),
        out_shape=out_shape,
        input_output_aliases=aliases,
        scratch_shapes=[pltpu.VMEM((U0 + tm, conv_dim), F32)],
        compiler_params=_params("arbitrary"),
        name="inproj_main",
    )(*args)


def _inproj_side(x, w, cw, cos, sin, pre1, pre2, sel1, sel2, *, layer):
    rows, _ = x.shape
    conv_dim = cw.shape[2]
    args = (x, w, cw, cos, sin, pre1, pre2, sel1, sel2)
    out = jax.ShapeDtypeStruct((rows, conv_dim), F32)
    return pl.pallas_call(
        functools.partial(_inproj_side_kernel, tm=rows, conv_dim=conv_dim),
        grid=(1,),
        in_specs=[_of_layer(a, layer) if a is w or a is cw else _full(a.shape) for a in args],
        out_specs=tuple(_full(out.shape) for _ in range(5)),
        out_shape=(out,) * 5,
        scratch_shapes=[pltpu.VMEM((U0 + rows, conv_dim), F32)],
        compiler_params=_params("arbitrary"),
        name="inproj_side",
    )(*args)


def _prompt_attn_kernel(qi_ref, kj_ref, q_ref, kt_ref, v_ref, kmt_ref, vm_ref,
                        lq1, lk1, lq2, lk2, sg_ref, o_ref, m_sc, acc_sc, *, tq, tk, lam_init):
    step = pl.program_id(1)
    i = qi_ref[step]
    j = kj_ref[step]

    def head_slices(h):
        return slice(h * HEAD_DIM, (h + 1) * HEAD_DIM), slice((h // 2) * V_DIM, (h // 2 + 1) * V_DIM)

    def pv(p, v_blk):
        ones = jnp.ones((v_blk.shape[0], LANES), BF16)
        return jnp.dot(p.astype(BF16), jnp.concatenate([v_blk, ones], axis=1), preferred_element_type=F32)

    @pl.when(j == 0)
    def _():
        col = lax.broadcasted_iota(jnp.int32, (tq, LANES), 1)
        for h in range(N_SUB):
            qk, vv = head_slices(h)
            s = jnp.dot(q_ref[:, qk], kmt_ref[qk, :], preferred_element_type=F32)
            s = jnp.where(col < N_META, s, NEG_INF)
            m = jnp.max(s, axis=1, keepdims=True)
            m_sc[h] = jnp.broadcast_to(m, (tq, LANES))
            acc_sc[h] = pv(jnp.exp2(s - m), vm_ref[:, vv])

    def process(rows, triangular, sub):
        n_rows = rows.stop - rows.start
        if isinstance(sub, int):
            keys = slice(sub * tk, (sub + 1) * tk)
        else:
            keys = pl.ds(pl.multiple_of(sub * tk, tk), tk)
        if triangular:
            row = lax.broadcasted_iota(jnp.int32, (n_rows, tk), 0)
            col = lax.broadcasted_iota(jnp.int32, (n_rows, tk), 1)
            visible = col <= row
        for h in range(N_SUB):
            qk, vv = head_slices(h)
            s = jnp.dot(q_ref[rows, qk], kt_ref[sub, qk, :], preferred_element_type=F32)
            if triangular:
                s = jnp.where(visible, s, NEG_INF)
            m_prev = m_sc[h, rows]
            m_new = jnp.maximum(m_prev, jnp.max(s, axis=1, keepdims=True))
            alpha = jnp.exp2(m_prev - m_new)
            p = jnp.exp2(s - _lanes(m_new, tk))
            m_sc[h, rows] = m_new
            acc_sc[h, rows] = acc_sc[h, rows] * _lanes(alpha, V_DIM + LANES) + pv(p, v_ref[keys, vv])

    per_q = tq // tk

    @pl.when(j < i)
    def _():
        def body(sub, carry):
            process(slice(0, tq), False, sub)
            return carry
        lax.fori_loop(0, per_q, body, 0)

    @pl.when(j == i)
    def _():
        for jl in range(per_q):
            process(slice(jl * tk, (jl + 1) * tk), True, jl)
            if jl + 1 < per_q:
                process(slice((jl + 1) * tk, tq), False, jl)
        lam = _lam(lq1, lk1, lq2, lk2, lam_init)
        g = sg_ref[...]
        for hv in range(N_HEADS):
            a1 = acc_sc[2 * hv]
            a2 = acc_sc[2 * hv + 1]
            o1 = a1[:, :V_DIM] / a1[:, V_DIM:]
            o2 = a2[:, :V_DIM] / a2[:, V_DIM:]
            o_ref[:, hv * V_DIM:(hv + 1) * V_DIM] = _diff_norm(o1, o2, lam, g, lam_init).astype(o_ref.dtype)


def _prompt_attention(q, kt, v, kt_meta, v_meta, lam_vecs, sub_g, *, seq, tq, tk, layer, lam_init):
    rows, attn_dim = q.shape
    n_seq = rows // seq
    nq = seq // tq
    per_q = tq // tk
    assert kt.shape == (n_seq, seq // tk, attn_dim, tk)
    pairs = [(i, j) for i in range(nq) for j in range(i + 1)]
    qi = jnp.asarray([p[0] for p in pairs], jnp.int32)
    kj = jnp.asarray([p[1] for p in pairs], jnp.int32)
    q_blk = pl.BlockSpec((tq, attn_dim), lambda b, s, qi, kj: (b * nq + qi[s], 0))
    v_blk = pl.BlockSpec((tq, attn_dim), lambda b, s, qi, kj: (b * nq + kj[s], 0))
    kt_blk = pl.BlockSpec((None, per_q, attn_dim, tk), lambda b, s, qi, kj: (b, kj[s], 0, 0))
    small = [_full(kt_meta.shape), _full(v_meta.shape)] + [_of_layer(a, layer) for a in (*lam_vecs, sub_g)]
    grid_spec = pltpu.PrefetchScalarGridSpec(
        num_scalar_prefetch=2,
        grid=(n_seq, len(pairs)),
        in_specs=[q_blk, kt_blk, v_blk] + small,
        out_specs=q_blk,
        scratch_shapes=[pltpu.VMEM((N_SUB, tq, LANES), F32), pltpu.VMEM((N_SUB, tq, V_DIM + LANES), F32)],
    )
    return pl.pallas_call(
        functools.partial(_prompt_attn_kernel, tq=tq, tk=tk, lam_init=lam_init),
        grid_spec=grid_spec,
        out_shape=jax.ShapeDtypeStruct((rows, attn_dim), BF16),
        compiler_params=_params("parallel", "arbitrary"),
        name="prompt_attn",
    )(qi, kj, q, kt, v, kt_meta, v_meta, *lam_vecs, sub_g)


def _side_attn_kernel(pt_ref, q_ref, kn_ref, vn_ref, *rest, n, pages_per_step, lam_init):
    del pt_ref
    k_refs = rest[:pages_per_step]
    v_refs = rest[pages_per_step:2 * pages_per_step]
    lq1, lk1, lq2, lk2, sg_ref, o_ref, qbd_sc, m_sc, l_sc, acc_sc = rest[2 * pages_per_step:]
    g = pl.program_id(1)
    rows = N_SUB * n
    attn_dim = q_ref.shape[1]

    @pl.when(g == 0)
    def _():
        qt = jnp.concatenate([q_ref[...]] * N_SUB, axis=0)
        row = lax.broadcasted_iota(jnp.int32, (rows, attn_dim), 0)
        lane = lax.broadcasted_iota(jnp.int32, (rows, attn_dim), 1)
        own = (row // n) == (lane // HEAD_DIM)
        qbd_sc[...] = jnp.where(own, qt, 0.0).astype(qbd_sc.dtype)
        m_sc[...] = jnp.full(m_sc.shape, NEG_INF, F32)
        l_sc[...] = jnp.zeros(l_sc.shape, F32)
        acc_sc[...] = jnp.zeros(acc_sc.shape, F32)

    def update(s, value_block):
        p, alpha, m_new, l_new = _online_softmax(s, m_sc[...], l_sc[...])
        m_sc[...] = m_new
        l_sc[...] = l_new
        for hv in range(N_HEADS):
            pair = slice(2 * hv * n, 2 * (hv + 1) * n)
            acc = acc_sc[pair, :] * alpha[pair]
            for c in range(s.shape[1] // LANES):
                acc = acc + jnp.dot(p[pair, c * LANES:(c + 1) * LANES].astype(BF16),
                                    value_block(hv, c).astype(BF16), preferred_element_type=F32)
            acc_sc[pair, :] = acc

    if pages_per_step:
        qbd = qbd_sc[...]
        s = jnp.concatenate([jnp.dot(qbd, k_ref[...].astype(BF16), preferred_element_type=F32)
                             for k_ref in k_refs], axis=1)
        update(s, lambda hv, c: v_refs[c][pl.ds(hv, LANES, stride=N_HEADS), :])

    @pl.when(g == pl.num_programs(1) - 1)
    def _():
        pad = jnp.zeros((LANES - n, attn_dim), F32)
        kn = jnp.concatenate([kn_ref[...], pad], axis=0).astype(BF16)
        vn = jnp.concatenate([vn_ref[...], pad], axis=0)
        row = lax.broadcasted_iota(jnp.int32, (rows, LANES), 0)
        col = lax.broadcasted_iota(jnp.int32, (rows, LANES), 1)
        s = jnp.where(col <= (row % n), _dot_nt(qbd_sc[...], kn), NEG_INF)
        update(s, lambda hv, c: vn[:, hv * V_DIM:(hv + 1) * V_DIM])
        lam = _lam(lq1, lk1, lq2, lk2, lam_init)
        gain = sg_ref[...]
        out = acc_sc[...] / l_sc[...]
        for hv in range(N_HEADS):
            r0 = 2 * hv * n
            o_ref[:, hv * V_DIM:(hv + 1) * V_DIM] = _diff_norm(out[r0:r0 + n], out[r0 + n:r0 + 2 * n],
                                                               lam, gain, lam_init)


def _side_attention(q, k_new, v_new, lam_vecs, sub_g, *, n, first_row, rows, layer, lam_init,
                    cache=None, pages_per_step=0):
    attn_dim = q.shape[1]
    n_seq = rows // n
    first_blk = first_row // n
    tok_blk = pl.BlockSpec((n, attn_dim), lambda b, g, pt: (first_blk + b, 0))
    out_blk = pl.BlockSpec((n, attn_dim), lambda b, g, pt: (b, 0))
    small = [_of_layer(a, layer) for a in (*lam_vecs, sub_g)]
    if cache is None:
        page_table = jnp.zeros((1,), jnp.int32)
        steps, kv_specs, kv_args = 1, [], []
    else:
        cache_k, cache_v, page_table = cache
        n_pages = page_table.shape[1]
        steps = n_pages // pages_per_step
        page_table = page_table.reshape(-1)

        def page_blk(jj):
            return pl.BlockSpec(
                (None, None) + cache_k.shape[2:],
                lambda b, g, pt: (layer, pt[b * n_pages + g * pages_per_step + jj], 0, 0))

        kv_specs = [page_blk(jj) for jj in range(pages_per_step)] * 2
        kv_args = [cache_k] * pages_per_step + [cache_v] * pages_per_step
    grid_spec = pltpu.PrefetchScalarGridSpec(
        num_scalar_prefetch=1,
        grid=(n_seq, steps),
        in_specs=[tok_blk, tok_blk, tok_blk] + kv_specs + small,
        out_specs=out_blk,
        scratch_shapes=[pltpu.VMEM((N_SUB * n, attn_dim), BF16), pltpu.VMEM((N_SUB * n, LANES), F32),
                        pltpu.VMEM((N_SUB * n, LANES), F32), pltpu.VMEM((N_SUB * n, V_DIM), F32)],
    )
    return pl.pallas_call(
        functools.partial(_side_attn_kernel, n=n, pages_per_step=pages_per_step if cache else 0,
                          lam_init=lam_init),
        grid_spec=grid_spec,
        out_shape=jax.ShapeDtypeStruct((rows, attn_dim), F32),
        compiler_params=_params("parallel", "arbitrary"),
        name="sample_attn" if cache else "meta_attn",
    )(page_table, q, k_new, v_new, *kv_args, *lam_vecs, sub_g)


def _post_kernel(yc_ref, ya_ref, x_ref, wo_ref, wgu_ref, wd_ref, g1, b1, g2, b2,
                 o_ref, x1_sc, x1b_sc, *, alpha, conv_dim):
    d_ff = wd_ref.shape[0]
    mix = jnp.dot(yc_ref[...].astype(BF16), wo_ref[0:conv_dim, :], preferred_element_type=F32)
    mix = mix + jnp.dot(ya_ref[...].astype(BF16), wo_ref[conv_dim:, :], preferred_element_type=F32)
    x1 = _layernorm(alpha * x_ref[...] + mix, g1[...], b1[...])
    x1_sc[...] = x1
    x1b_sc[...] = x1.astype(BF16)

    ffn = None
    for c0 in range(0, d_ff, FF_CHUNK):
        x1b = x1b_sc[...]
        gate = jnp.dot(x1b, wgu_ref[:, c0:c0 + FF_CHUNK], preferred_element_type=F32)
        up = jnp.dot(x1b, wgu_ref[:, d_ff + c0:d_ff + c0 + FF_CHUNK], preferred_element_type=F32)
        hidden = (gate / (1.0 + jnp.exp(-gate))) * up
        part = jnp.dot(hidden.astype(BF16), wd_ref[c0:c0 + FF_CHUNK, :], preferred_element_type=F32)
        ffn = part if ffn is None else ffn + part
    o_ref[...] = _layernorm(alpha * x1_sc[...] + ffn, g2[...], b2[...])


def _post_attention(yc, ya, x, wo, wgu, wd, g1, b1, g2, b2, *, tm, alpha, layer):
    rows, d_model = x.shape
    conv_dim = yc.shape[1]
    assert wd.shape[1] % FF_CHUNK == 0
    row_blk = lambda width: pl.BlockSpec((tm, width), lambda i: (i, 0))
    consts = (wo, wgu, wd, g1, b1, g2, b2)
    return pl.pallas_call(
        functools.partial(_post_kernel, alpha=alpha, conv_dim=conv_dim),
        grid=(rows // tm,),
        in_specs=[row_blk(conv_dim), row_blk(ya.shape[1]), row_blk(d_model)] + [_of_layer(a, layer) for a in consts],
        out_specs=row_blk(d_model),
        out_shape=jax.ShapeDtypeStruct((rows, d_model), F32),
        scratch_shapes=[pltpu.VMEM((tm, d_model), F32), pltpu.VMEM((tm, d_model), BF16)],
        compiler_params=_params("parallel"),
        name="post_main" if rows > tm else "post_side",
    )(yc, ya, x, *consts)


def _rope_tables(pos):
    inv = 1.0 / (ROPE_THETA ** (jnp.arange(0, HEAD_DIM, 2, dtype=F32) / HEAD_DIM))
    ang = pos.astype(F32)[:, None] * inv[None, :]
    cos = jnp.cos(ang)
    sin = jnp.sin(ang)
    return jnp.concatenate([cos] * 4, axis=-1), jnp.concatenate([-sin, sin] * 2, axis=-1)


def _row_tile(n, target):
    t = min(n, target)
    while n % t:
        t //= 2
    return t


def kernel(x_prompt, x_sample, cache_k, cache_v, state_conv, page_table, meta_tokens, w_in, conv_w, w_out, lambda_q1, lambda_k1, lambda_q2, lambda_k2, subln_g, ln1_g, ln1_b, ln2_g, ln2_b, w_gate_up, w_down):
    batch, seq, d_model = x_prompt.shape
    dec_batch, dec_seq, _ = x_sample.shape
    depth = w_in.shape[0]
    conv_dim = conv_w.shape[2]
    attn_dim = N_SUB * HEAD_DIM
    d_ff = w_down.shape[1]
    n_pool, page_size = cache_k.shape[1], cache_k.shape[2]
    past_len = page_table.shape[1] * page_size
    alpha = (2 * depth) ** 0.25
    n_sample = dec_batch * dec_seq
    n_side = n_sample + N_META
    tm = _row_tile(seq, ROW_TILE)
    tq = _row_tile(seq, ATTN_TQ)
    tk = _row_tile(tq, ATTN_TK)
    pages_per_step = _row_tile(page_table.shape[1], PAGES_PER_STEP)
    assert page_size == LANES and attn_dim == conv_dim
    assert tk == tm
    assert n_sample % N_META == 0

    w_in_b = w_in.astype(BF16)
    k_col = 3 * conv_dim + attn_dim
    w_kt = w_in_b[:, :, k_col:k_col + attn_dim].transpose(0, 2, 1)
    w_out_b = w_out.astype(BF16)
    w_gate_up_b = w_gate_up.astype(BF16)
    w_down_b = w_down.astype(BF16)
    per_layer_row = lambda a: a[:, None, :]
    lam_vecs = tuple(per_layer_row(a) for a in (lambda_q1, lambda_k1, lambda_q2, lambda_k2))
    sub_g = per_layer_row(subln_g)
    norms = tuple(per_layer_row(a) for a in (ln1_g, ln1_b, ln2_g, ln2_b))
    cache_kt = cache_k.transpose(0, 1, 3, 4, 2).reshape(depth, n_pool, attn_dim, page_size)
    cache_vi = cache_v.reshape(depth, n_pool, page_size * N_HEADS, V_DIM)

    cos_m, sin_m = _rope_tables(N_META + jnp.arange(seq, dtype=jnp.int32))
    cost_m, sint_m = cos_m[:, :HEAD_DIM].T, sin_m[:, :HEAD_DIM].T
    pos_side = jnp.concatenate([jnp.tile(past_len + jnp.arange(dec_seq, dtype=jnp.int32), dec_batch),
                                jnp.arange(N_META, dtype=jnp.int32)])
    cos_s, sin_s = _rope_tables(pos_side)

    j_in_seq = jnp.concatenate([jnp.tile(jnp.arange(dec_seq, dtype=jnp.int32), dec_batch),
                                jnp.arange(N_META, dtype=jnp.int32)])
    sel1 = (j_in_seq < 1).astype(F32)[:, None]
    sel2 = (j_in_seq < 2).astype(F32)[:, None]

    def prefixes(state):
        z = jnp.zeros((dec_batch, dec_seq, conv_dim), F32)
        p1 = z.at[:, 0].set(state[:, 1]).reshape(n_sample, conv_dim)
        p2 = z.at[:, 0].set(state[:, 0]).at[:, 1].set(state[:, 1]).reshape(n_sample, conv_dim)
        zm = jnp.zeros((N_META, conv_dim), F32)
        return jnp.concatenate([p1, zm]), jnp.concatenate([p2, zm])

    x_main = x_prompt.reshape(batch * seq, d_model)
    x_side = jnp.concatenate([x_sample.reshape(n_sample, d_model), meta_tokens.astype(F32)])

    outs = {name: [] for name in ("cp", "ks", "vs", "cs")}
    prompt_cache, kt_tails, meta_v = None, [], []
    for l in range(depth):
        lam_init = 0.8 - 0.6 * math.exp(-0.3 * l)

        pre1, pre2 = prefixes(state_conv[l])
        yc_s, q_s, k_s, v_s, u_s = _inproj_side(x_side, w_in_b, conv_w, cos_s, sin_s, pre1, pre2, sel1, sel2,
                                                layer=l)
        k_meta, v_meta, u_meta = k_s[n_sample:], v_s[n_sample:], u_s[n_sample:]

        kmt = jnp.concatenate([k_meta, jnp.zeros((LANES - N_META, attn_dim), F32)]).T
        yc_m, q_m, kt_cache, v_cache, ktb_m, vb_m, cstate, kt_tail = _inproj_main(
            x_main, w_in_b, w_kt, conv_w, cos_m, sin_m, cost_m, sint_m,
            u_meta[N_META - (CONV_W - 1):], kmt, prompt_cache, seq=seq, tm=tm, layer=l, depth=depth)
        prompt_cache = (kt_cache, v_cache)
        kt_tails.append(kt_tail[:, :, :N_META])
        meta_v.append(v_meta.reshape(N_META * N_HEADS, V_DIM))

        pad = jnp.zeros((LANES - N_META, attn_dim), BF16)
        ya_m = _prompt_attention(q_m, ktb_m, vb_m, kmt.astype(BF16),
                                 jnp.concatenate([v_meta.astype(BF16), pad]),
                                 lam_vecs, sub_g, seq=seq, tq=tq, tk=tk, layer=l, lam_init=lam_init)
        side_attend = functools.partial(_side_attention, q_s, k_s, v_s, lam_vecs, sub_g, layer=l, lam_init=lam_init)
        ya_sample = side_attend(n=dec_seq, first_row=0, rows=n_sample,
                                cache=(cache_kt, cache_vi, page_table), pages_per_step=pages_per_step)
        ya_meta = side_attend(n=N_META, first_row=n_sample, rows=N_META)
        ya_s = jnp.concatenate([ya_sample, ya_meta])

        post = functools.partial(_post_attention, wo=w_out_b, wgu=w_gate_up_b, wd=w_down_b, g1=norms[0],
                                 b1=norms[1], g2=norms[2], b2=norms[3], alpha=alpha, layer=l)
        x_main = post(yc_m, ya_m, x_main, tm=tm)
        x_side = post(yc_s, ya_s, x_side, tm=n_side)

        outs["cp"].append(cstate)
        outs["ks"].append(k_s[:n_sample].reshape(dec_batch, dec_seq, N_SUB, HEAD_DIM))
        outs["vs"].append(v_s[:n_sample].reshape(dec_batch, dec_seq, N_HEADS, V_DIM))
        outs["cs"].append(u_s[:n_sample].reshape(dec_batch, dec_seq, conv_dim)[:, dec_seq - (CONV_W - 1):])

    kt_cache, v_cache = prompt_cache
    new_k_prompt = lax.dynamic_update_slice(kt_cache, jnp.stack(kt_tails), (0, 0, 0, seq)).reshape(
        depth, batch, N_SUB, HEAD_DIM, N_META + seq).transpose(0, 1, 4, 2, 3)
    meta_v_b = jnp.broadcast_to(jnp.stack(meta_v)[:, None], (depth, batch, N_META * N_HEADS, V_DIM))
    new_v_prompt = lax.dynamic_update_slice(v_cache, meta_v_b, (0, 0, 0, 0)).reshape(
        depth, batch, N_META + seq, N_HEADS, V_DIM)

    y_prompt = x_main.reshape(batch, seq, d_model)
    y_sample = x_side[:n_sample].reshape(dec_batch, dec_seq, d_model)
    return (y_prompt, y_sample, new_k_prompt, new_v_prompt, jnp.stack(outs["cp"]),
            jnp.stack(outs["ks"]), jnp.stack(outs["vs"]), jnp.stack(outs["cs"]))
```
